```python
import jax
import jax.numpy as jnp
from jax import lax
import numpy as np

D_MODEL = 2048
BATCH = 4
SEQ = 2048
DEPTH = 2

HEAD_DIM = 128
NSA_HEADS = D_MODEL // (2 * HEAD_DIM)
NSA_KV_HEADS = NSA_HEADS // 4
RET_HEADS = D_MODEL // (2 * HEAD_DIM)
NSA_W = NSA_HEADS * HEAD_DIM
NSA_KV_W = NSA_KV_HEADS * HEAD_DIM
RET_W = RET_HEADS * HEAD_DIM
CMP_BLOCK = 32
CMP_STRIDE = 16
SLC_BLOCK = 64
SLC_TOPK = 16
WINDOW = 512
NSA_Q_CHUNK = 64
WIN_Q_BLOCK = 128
RET_CHUNK = 128
N_MEM = 256
XA_HEADS = 4
XA_HEAD_DIM = D_MODEL // XA_HEADS
PEER_HEADS = 8
PEER_NKEYS = 128
PEER_N_EXPERTS = PEER_NKEYS * PEER_NKEYS
PEER_QUERY_DIM = 256
PEER_HALF = PEER_QUERY_DIM // 2
PEER_TOPK = 16
PEER_TOKEN_CHUNK = 128
ROPE_THETA = 10000.0
LN_EPS = 1e-5
GN_EPS = 1e-5
ALPHA = (2 * DEPTH) ** 0.25
BETA = (8 * DEPTH) ** -0.25
NEG_INF = -1e30
FORCE_SCORE = 1e9
IN_SIZES = (NSA_W, NSA_KV_W, NSA_KV_W, NSA_KV_W, NSA_KV_W, NSA_KV_W, NSA_KV_W, 3 * NSA_HEADS, RET_W, RET_W, RET_W, RET_W)
IN_COL_SCALE = (1.0, 1.0, BETA, 1.0, BETA, 1.0, BETA, 1.0, 1.0, 1.0, BETA, 1.0)
P_IN = sum(IN_SIZES)

kernel_name = 'hybrid_nsa_retention_peer_block'


def layer_norm(x, g, b):
    xf = x.astype(jnp.float32)
    mu = xf.mean(-1, keepdims=True)
    var = ((xf - mu) ** 2).mean(-1, keepdims=True)
    return ((xf - mu) * lax.rsqrt(var + LN_EPS) * g + b).astype(x.dtype)


def rope(x):
    S, Dh = x.shape[1], x.shape[-1]
    inv = 1.0 / (ROPE_THETA ** (jnp.arange(0, Dh, 2, dtype=jnp.float32) / Dh))
    ang = jnp.arange(S, dtype=jnp.float32)[:, None] * inv[None, :]
    cos = jnp.cos(ang)[:, None, :]
    sin = jnp.sin(ang)[:, None, :]
    xf = x.astype(jnp.float32)
    x1, x2 = xf[..., :Dh // 2], xf[..., Dh // 2:]
    return jnp.concatenate([x1 * cos - x2 * sin, x1 * sin + x2 * cos], -1).astype(x.dtype)


def nsa_attention(q, kc, vc, ks, vs, kw, vw, gate_logits, cmp_pos, cmp_w1, cmp_w2):
    B, S, H, Dh = q.shape
    Hkv = kc.shape[2]
    G = H // Hkv
    f32 = jnp.float32
    scale = Dh ** -0.5
    tpos = jnp.arange(S)

    n_cmp = (S - CMP_BLOCK) // CMP_STRIDE + 1
    blk_start = jnp.arange(n_cmp) * CMP_STRIDE
    gidx = blk_start[:, None] + jnp.arange(CMP_BLOCK)[None, :]

    def compress(t, i):
        tb = t[:, gidx] + cmp_pos[i][None, None, :, None, :]
        tb = tb.transpose(0, 1, 3, 2, 4).reshape(B, n_cmp, Hkv, CMP_BLOCK * Dh)
        return jax.nn.gelu(tb @ cmp_w1[i], approximate=False) @ cmp_w2[i]

    k_cmp = compress(kc, 0)
    v_cmp = compress(vc, 1)
    qg = q.reshape(B, S, Hkv, G, Dh)
    s_c = jnp.einsum('bsgrd,bngd->bgrsn', qg, k_cmp).astype(f32) * scale
    vis_c = (blk_start + CMP_BLOCK - 1)[None, :] <= tpos[:, None]
    p_c = jnp.where(vis_c, jax.nn.softmax(jnp.where(vis_c, s_c, NEG_INF), axis=-1), 0.0)
    o_c = jnp.einsum('bgrsn,bngd->bsgrd', p_c.astype(q.dtype), v_cmp).reshape(B, S, H, Dh)

    n_slc = S // SLC_BLOCK
    slc_start = jnp.arange(n_slc) * SLC_BLOCK
    overlap = ((blk_start[:, None] < (slc_start + SLC_BLOCK)[None, :]) &
               ((blk_start + CMP_BLOCK)[:, None] > slc_start[None, :])).astype(f32)
    imp = jnp.einsum('bgrsn,nj->bgsj', p_c, overlap)
    cur = tpos // SLC_BLOCK
    jb = jnp.arange(n_slc)
    forced = (jb[None, :] == 0) | (jb[None, :] == cur[:, None]) | (jb[None, :] == cur[:, None] - 1)
    score = jnp.where(forced, FORCE_SCORE, imp)
    score = jnp.where(slc_start[None, :] <= tpos[:, None], score, -1.0)
    n_top = min(SLC_TOPK, n_slc)
    _, sel = lax.top_k(score, n_top)

    q_r = rope(q)
    ks_r = rope(ks)
    kw_r = rope(kw)

    ks_blk = ks_r.reshape(B, n_slc, SLC_BLOCK, Hkv, Dh).transpose(0, 3, 1, 2, 4)
    vs_blk = vs.reshape(B, n_slc, SLC_BLOCK, Hkv, Dh).transpose(0, 3, 1, 2, 4)
    C = NSA_Q_CHUNK
    n_ch = S // C
    q_ch = q_r.reshape(B, n_ch, C, Hkv, G, Dh).transpose(1, 0, 3, 2, 4, 5)
    sel_ch = sel.reshape(B, Hkv, n_ch, C, n_top).transpose(2, 0, 1, 3, 4)
    pos_ch = tpos.reshape(n_ch, C)
    bi = jnp.arange(B)[:, None, None, None]
    gi = jnp.arange(Hkv)[None, :, None, None]

    def sel_chunk(args):
        qc, sc, pc = args
        kg = ks_blk[bi, gi, sc]
        vg = vs_blk[bi, gi, sc]
        kpos = sc[..., None] * SLC_BLOCK + jnp.arange(SLC_BLOCK)
        ok = (kpos <= pc[None, None, :, None, None])[:, :, :, None]
        s = jnp.einsum('bgcrd,bgcnld->bgcrnl', qc, kg).astype(f32) * scale
        s = jnp.where(ok, s, NEG_INF).reshape(B, Hkv, C, G, n_top * SLC_BLOCK)
        p = jax.nn.softmax(s, axis=-1).reshape(B, Hkv, C, G, n_top, SLC_BLOCK).astype(qc.dtype)
        return jnp.einsum('bgcrnl,bgcnld->bgcrd', p, vg)

    o_s = lax.map(sel_chunk, (q_ch, sel_ch, pos_ch))
    o_s = o_s.transpose(1, 0, 3, 2, 4, 5).reshape(B, S, H, Dh)

    WQ = WIN_Q_BLOCK
    n_qb = S // WQ
    span = WINDOW + WQ
    pad = ((0, 0), (WINDOW, 0), (0, 0), (0, 0))
    kw_pad = jnp.pad(kw_r, pad)
    vw_pad = jnp.pad(vw, pad)
    widx = jnp.arange(n_qb)[:, None] * WQ + jnp.arange(span)[None, :]
    kwin = kw_pad[:, widx]
    vwin = vw_pad[:, widx]
    kpos = widx - WINDOW
    qpos = tpos.reshape(n_qb, WQ)
    ok_w = ((kpos[:, None, :] <= qpos[:, :, None]) &
            (qpos[:, :, None] - kpos[:, None, :] < WINDOW) &
            (kpos[:, None, :] >= 0))
    qb = q_r.reshape(B, n_qb, WQ, Hkv, G, Dh)
    s_w = jnp.einsum('bqtgrd,bqkgd->bqgrtk', qb, kwin).astype(f32) * scale
    s_w = jnp.where(ok_w[None, :, None, None], s_w, NEG_INF)
    p_w = jax.nn.softmax(s_w, axis=-1).astype(q.dtype)
    o_w = jnp.einsum('bqgrtk,bqkgd->bqtgrd', p_w, vwin).reshape(B, S, H, Dh)

    g = jax.nn.sigmoid(gate_logits.astype(f32)).astype(q.dtype)
    return g[..., 0:1] * o_c + g[..., 1:2] * o_s + g[..., 2:3] * o_w


def retention(q, k, v, gate, gn_g, gn_b):
    B, S, H, Dh = q.shape
    f32 = jnp.float32
    q = rope(q).astype(f32)
    k = rope(k).astype(f32) * (Dh ** -0.5)
    v = v.astype(f32)
    log_g = jnp.log(1.0 - 2.0 ** (-5.0 - jnp.arange(H, dtype=f32)))
    C = RET_CHUNK
    n = S // C
    i = jnp.arange(C, dtype=f32)
    diff = i[:, None] - i[None, :]
    causal = diff >= 0
    dmask = jnp.where(causal[None], jnp.exp(jnp.where(causal, diff, 0.0)[None] * log_g[:, None, None]), 0.0)
    xi = jnp.exp((i[None, :] + 1.0) * log_g[:, None])
    zeta = jnp.exp((C - 1.0 - i[None, :]) * log_g[:, None])
    cdec = jnp.exp(C * log_g)

    def to_ch(t):
        return t.reshape(B, n, C, H, t.shape[-1]).transpose(1, 0, 3, 2, 4)

    def step(R, inp):
        qi, ki, vi = inp
        inner = jnp.einsum('bhid,bhjd->bhij', qi, ki) * dmask
        o = jnp.einsum('bhij,bhjv->bhiv', inner, vi) + jnp.einsum('bhid,bhdv->bhiv', qi, R) * xi[None, :, :, None]
        R = R * cdec[None, :, None, None] + jnp.einsum('bhjd,bhjv->bhdv', ki * zeta[None, :, :, None], vi)
        return R, o

    R0 = jnp.zeros((B, H, Dh, v.shape[-1]), f32)
    _, ys = lax.scan(step, R0, (to_ch(q), to_ch(k), to_ch(v)))
    y = ys.transpose(1, 0, 3, 2, 4).reshape(B, S, H, -1)
    mu = y.mean(-1, keepdims=True)
    var = ((y - mu) ** 2).mean(-1, keepdims=True)
    y = ((y - mu) * lax.rsqrt(var + GN_EPS)).reshape(B, S, -1) * gn_g + gn_b
    return (jax.nn.silu(gate.astype(f32)) * y).astype(gate.dtype)


def hybrid_mixer(x, w_in, b_gate, cmp_pos, cmp_w1, cmp_w2, gn_g, gn_b, w_out):
    B, S, _ = x.shape
    proj = x @ w_in
    offs = np.cumsum(IN_SIZES)[:-1].tolist()
    (q, kc, vc, ks, vs, kw, vw, gl, rq, rk, rv, rg) = jnp.split(proj, offs, axis=-1)

    def heads(t, h):
        return t.reshape(B, S, h, HEAD_DIM)

    o_nsa = nsa_attention(heads(q, NSA_HEADS), heads(kc, NSA_KV_HEADS), heads(vc, NSA_KV_HEADS),
                          heads(ks, NSA_KV_HEADS), heads(vs, NSA_KV_HEADS),
                          heads(kw, NSA_KV_HEADS), heads(vw, NSA_KV_HEADS),
                          (gl + b_gate).reshape(B, S, NSA_HEADS, 3), cmp_pos, cmp_w1, cmp_w2)
    o_ret = retention(heads(rq, RET_HEADS), heads(rk, RET_HEADS), heads(rv, RET_HEADS), rg, gn_g, gn_b)
    return jnp.concatenate([o_nsa.reshape(B, S, NSA_W), o_ret], axis=-1) @ w_out


def memory_cross_attention(x, mem, wq, wk, wv, wo):
    B, S, D = x.shape
    M = mem.shape[1]
    q = (x @ wq).reshape(B, S, XA_HEADS, XA_HEAD_DIM)
    k = (mem @ wk).reshape(B, M, XA_HEADS, XA_HEAD_DIM)
    v = (mem @ wv).reshape(B, M, XA_HEADS, XA_HEAD_DIM)
    s = jnp.einsum('bshd,bmhd->bhsm', q, k).astype(jnp.float32) * (XA_HEAD_DIM ** -0.5)
    p = jax.nn.softmax(s, axis=-1).astype(x.dtype)
    o = jnp.einsum('bhsm,bmhd->bshd', p, v).reshape(B, S, D)
    return o @ wo


def peer_ffn(x, w_q, sub_keys, u_tab, v_tab):
    B, S, D = x.shape
    T = B * S
    xt = x.reshape(T, D)
    q = (xt @ w_q).reshape(T, PEER_HEADS, 2, PEER_HALF)
    s = jnp.einsum('thpk,hpnk->thpn', q, sub_keys).astype(jnp.float32)
    s1, i1 = lax.top_k(s[:, :, 0], PEER_TOPK)
    s2, i2 = lax.top_k(s[:, :, 1], PEER_TOPK)
    cand = (s1[..., :, None] + s2[..., None, :]).reshape(T, PEER_HEADS, PEER_TOPK * PEER_TOPK)
    top, pos = lax.top_k(cand, PEER_TOPK)
    e = (jnp.take_along_axis(i1, pos // PEER_TOPK, axis=-1) * PEER_NKEYS +
         jnp.take_along_axis(i2, pos % PEER_TOPK, axis=-1))
    g = jax.nn.softmax(top, axis=-1).astype(x.dtype)
    C = PEER_TOKEN_CHUNK
    n_ch = T // C

    def chunk(args):
        xc, ec, gc = args
        h = jax.nn.gelu(jnp.einsum('td,thkd->thk', xc, u_tab[ec]), approximate=False)
        return jnp.einsum('thk,thkd->td', gc * h, v_tab[ec])

    out = lax.map(chunk, (xt.reshape(n_ch, C, D), e.reshape(n_ch, C, PEER_HEADS, PEER_TOPK),
                          g.reshape(n_ch, C, PEER_HEADS, PEER_TOPK)))
    return out.reshape(B, S, D)


def setup_inputs(seed: int = 0) -> dict:
    key = jax.random.key(seed)
    k = jax.random.split(key, 24)
    L, D, f32 = DEPTH, D_MODEL, jnp.float32

    def nrm(kk, shape, scale):
        return jax.random.normal(kk, shape, f32) * scale

    col_scale = jnp.concatenate([jnp.full((n,), s, f32) for n, s in zip(IN_SIZES, IN_COL_SCALE)])
    return {
        'x': nrm(k[0], (BATCH, SEQ, D), 1.0),
        'mem': nrm(k[1], (BATCH, N_MEM, D), 1.0),
        'w_in': nrm(k[2], (L, D, P_IN), D ** -0.5) * col_scale,
        'b_gate': nrm(k[3], (L, 3 * NSA_HEADS), 0.01),
        'cmp_pos': nrm(k[4], (L, 2, CMP_BLOCK, HEAD_DIM), 0.02),
        'cmp_w1': nrm(k[5], (L, 2, CMP_BLOCK * HEAD_DIM, HEAD_DIM), (CMP_BLOCK * HEAD_DIM) ** -0.5),
        'cmp_w2': nrm(k[6], (L, 2, HEAD_DIM, HEAD_DIM), HEAD_DIM ** -0.5),
        'ret_gn_g': 1.0 + nrm(k[7], (L, RET_W), 0.02),
        'ret_gn_b': nrm(k[8], (L, RET_W), 0.02),
        'w_mix_out': nrm(k[9], (L, D, D), BETA * D ** -0.5),
        'ln1_g': 1.0 + nrm(k[10], (L, D), 0.02),
        'ln1_b': nrm(k[11], (L, D), 0.02),
        'xa_wq': nrm(k[12], (L, D, D), D ** -0.5),
        'xa_wk': nrm(k[13], (L, D, D), D ** -0.5),
        'xa_wv': nrm(k[14], (L, D, D), BETA * D ** -0.5),
        'xa_wo': nrm(k[15], (L, D, D), BETA * D ** -0.5),
        'ln2_g': 1.0 + nrm(k[16], (L, D), 0.02),
        'ln2_b': nrm(k[17], (L, D), 0.02),
        'peer_wq': nrm(k[18], (L, D, PEER_HEADS * PEER_QUERY_DIM), D ** -0.5),
        'peer_sub_keys': nrm(k[19], (L, PEER_HEADS, 2, PEER_NKEYS, PEER_HALF), PEER_HALF ** -0.5),
        'peer_u': nrm(k[20], (L, PEER_N_EXPERTS, D), D ** -0.5),
        'peer_v': nrm(k[21], (L, PEER_N_EXPERTS, D), BETA * PEER_HEADS ** -0.5),
        'ln3_g': 1.0 + nrm(k[22], (L, D), 0.02),
        'ln3_b': nrm(k[23], (L, D), 0.02),
    }


def reference(x, mem, w_in, b_gate, cmp_pos, cmp_w1, cmp_w2, ret_gn_g, ret_gn_b, w_mix_out,
              ln1_g, ln1_b, xa_wq, xa_wk, xa_wv, xa_wo, ln2_g, ln2_b,
              peer_wq, peer_sub_keys, peer_u, peer_v, ln3_g, ln3_b):
    for l in range(DEPTH):
        h = hybrid_mixer(x, w_in[l], b_gate[l], cmp_pos[l], cmp_w1[l], cmp_w2[l],
                         ret_gn_g[l], ret_gn_b[l], w_mix_out[l])
        x = layer_norm(ALPHA * x + h, ln1_g[l], ln1_b[l])
        h = memory_cross_attention(x, mem, xa_wq[l], xa_wk[l], xa_wv[l], xa_wo[l])
        x = layer_norm(ALPHA * x + h, ln2_g[l], ln2_b[l])
        h = peer_ffn(x, peer_wq[l], peer_sub_keys[l], peer_u[l], peer_v[l])
        x = layer_norm(ALPHA * x + h, ln3_g[l], ln3_b[l])
    return x
```

```python
import functools

import numpy as np
import jax
import jax.numpy as jnp
from jax import lax
from jax.experimental import pallas as pl
from jax.experimental.pallas import tpu as pltpu

F32 = jnp.float32
BF16 = jnp.bfloat16
I32 = jnp.int32

HEAD_DIM = 128
LANES = 128
NSA_HEADS = 8
NSA_KV_HEADS = 2
NSA_GROUP = NSA_HEADS // NSA_KV_HEADS
RET_HEADS = 8
CMP_BLOCK = 32
CMP_STRIDE = 16
SLC_BLOCK = 64
SLC_TOPK = 16
WINDOW = 512
RET_CHUNK = 128
XA_HEADS = 4
PEER_HEADS = 8
PEER_NKEYS = 128
PEER_HALF = 128
PEER_TOPK = 16
ROPE_THETA = 10000.0
LN_EPS = 1e-5
GN_EPS = 1e-5
NEG_INF = -1e30
FORCE_SCORE = 1e9
VMEM_LIMIT = 56 * 1024 * 1024

CB_Q = 0
CB_KC, CB_VC, CB_KS, CB_VS, CB_KW, CB_VW = 8, 10, 12, 14, 16, 18
CB_GATE = 20
CB_RQ, CB_RK, CB_RV, CB_RG = 22, 30, 38, 46
P_PACKED = 54 * LANES


def _dot(a, b):
    return jnp.dot(a, b, preferred_element_type=F32)


def _dot_nt(a, b):
    return lax.dot_general(a, b, (((1,), (1,)), ((), ())), preferred_element_type=F32)


def _dot_tn(a, b):
    return lax.dot_general(a, b, (((0,), (0,)), ((), ())), preferred_element_type=F32)


def _gelu(x):
    return 0.5 * x * (1.0 + lax.erf(x * (2.0 ** -0.5)))


def _rope(x, cos, sin_signed):
    return x * cos + pltpu.roll(x, HEAD_DIM // 2, axis=1) * sin_signed


def _layer_norm(y, g, b):
    mu = jnp.mean(y, axis=-1, keepdims=True)
    d = y - mu
    var = jnp.mean(d * d, axis=-1, keepdims=True)
    return d * lax.rsqrt(var + LN_EPS) * g + b


def _softmax_rows(s):
    m = jnp.max(s, axis=-1, keepdims=True)
    e = jnp.exp(s - m)
    return e / jnp.sum(e, axis=-1, keepdims=True)


def _params(*sem):
    return pltpu.CompilerParams(dimension_semantics=sem, vmem_limit_bytes=VMEM_LIMIT)


def _mm_kernel(a_ref, w_ref, o_ref, a_bf):
    @pl.when(pl.program_id(1) == 0)
    def _():
        a_bf[...] = a_ref[...].astype(BF16)

    o_ref[...] = _dot(a_bf[...], w_ref[...]).astype(o_ref.dtype)


def matmul(a, w, tm, tn):
    M, K = a.shape
    N = w.shape[1]
    assert M % tm == 0 and N % tn == 0
    return pl.pallas_call(
        _mm_kernel,
        grid=(M // tm, N // tn),
        in_specs=[pl.BlockSpec((tm, K), lambda i, j: (i, 0)),
                  pl.BlockSpec((K, tn), lambda i, j: (0, j))],
        out_specs=pl.BlockSpec((tm, tn), lambda i, j: (i, j)),
        out_shape=jax.ShapeDtypeStruct((M, N), F32),
        scratch_shapes=[pltpu.VMEM((tm, K), BF16)],
        compiler_params=_params("parallel", "arbitrary"),
        name="matmul",
    )(a, w)


def _mm_ln_kernel(n_in, alpha, *refs):
    a_refs = refs[:n_in]
    w_ref, x_ref, g_ref, b_ref, o_ref = refs[n_in:]
    acc = None
    off = 0
    for a_ref in a_refs:
        k = a_ref.shape[1]
        part = _dot(a_ref[...].astype(BF16), w_ref[off:off + k, :])
        acc = part if acc is None else acc + part
        off += k
    o_ref[...] = _layer_norm(alpha * x_ref[...] + acc, g_ref[...], b_ref[...])


def matmul_residual_ln(a_list, w, x, g, b, alpha, tm):
    M, D = x.shape
    n_in = len(a_list)
    in_specs = [pl.BlockSpec((tm, a.shape[1]), lambda i: (i, 0)) for a in a_list]
    in_specs += [pl.BlockSpec(w.shape, lambda i: (0, 0)),
                 pl.BlockSpec((tm, D), lambda i: (i, 0)),
                 pl.BlockSpec((1, D), lambda i: (0, 0)),
                 pl.BlockSpec((1, D), lambda i: (0, 0))]
    return pl.pallas_call(
        functools.partial(_mm_ln_kernel, n_in, alpha),
        grid=(M // tm,),
        in_specs=in_specs,
        out_specs=pl.BlockSpec((tm, D), lambda i: (i, 0)),
        out_shape=jax.ShapeDtypeStruct((M, D), F32),
        compiler_params=_params("parallel"),
        name="matmul_residual_ln",
    )(*a_list, w, x, g.reshape(1, D), b.reshape(1, D))


def _compress_kernel(x_ref, pos_ref, w1_ref, w2_ref, o_ref):
    x = x_ref[...]
    half = x.shape[1]
    pos = pos_ref[...]
    lo = (x + pos[:, :half]).astype(BF16)
    hi = (x + pos[:, half:]).astype(BF16)
    p_lo = _dot(lo, w1_ref[:half, :])
    p_hi = _dot(hi, w1_ref[half:, :])
    h = p_lo + pltpu.roll(p_hi, x.shape[0] - 1, axis=0)
    o_ref[...] = _dot(_gelu(h).astype(BF16), w2_ref[...])


def nsa_compress(x16, pos, w1, w2):
    _, BG, n_blk, wd = x16.shape
    return pl.pallas_call(
        _compress_kernel,
        grid=(2, BG),
        in_specs=[pl.BlockSpec((None, None, n_blk, wd), lambda i, j: (i, j, 0, 0)),
                  pl.BlockSpec((None, 1, 2 * wd), lambda i, j: (i, 0, 0)),
                  pl.BlockSpec((None, 2 * wd, HEAD_DIM), lambda i, j: (i, 0, 0)),
                  pl.BlockSpec((None, HEAD_DIM, HEAD_DIM), lambda i, j: (i, 0, 0))],
        out_specs=pl.BlockSpec((None, None, n_blk, HEAD_DIM), lambda i, j: (i, j, 0, 0)),
        out_shape=jax.ShapeDtypeStruct((2, BG, n_blk, HEAD_DIM), F32),
        compiler_params=_params("parallel", "parallel"),
        name="nsa_compress",
    )(x16, pos, w1, w2)


def _cmp_attn_kernel(tq, q_ref, kc_ref, vc_ref, ov_ref, o_ref, sel_ref):
    qi = pl.program_id(2)
    scale = HEAD_DIM ** -0.5
    t = qi * tq + lax.broadcasted_iota(I32, (tq, LANES), 0)
    lane = lax.broadcasted_iota(I32, (tq, LANES), 1)
    vis = (lane * CMP_STRIDE + (CMP_BLOCK - 1)) <= t
    kc = kc_ref[...].astype(BF16)
    vc = vc_ref[...].astype(BF16)
    psum = jnp.zeros((tq, LANES), F32)
    for r in range(NSA_GROUP):
        q = q_ref[:, r * HEAD_DIM:(r + 1) * HEAD_DIM].astype(BF16)
        s = _dot_nt(q, kc) * scale
        s = jnp.where(vis, s, NEG_INF)
        m = jnp.max(s, axis=-1, keepdims=True)
        e = jnp.where(vis, jnp.exp(s - m), 0.0)
        l = jnp.sum(e, axis=-1, keepdims=True)
        p = e / jnp.where(l > 0.0, l, 1.0)
        o_ref[:, r * HEAD_DIM:(r + 1) * HEAD_DIM] = _dot(p.astype(BF16), vc)
        psum = psum + p
    p_hi = psum.astype(BF16)
    p_lo = (psum - p_hi.astype(F32)).astype(BF16)
    imp = _dot(p_hi, ov_ref[...]) + _dot(p_lo, ov_ref[...])
    n_slc = ov_ref.shape[0] * CMP_STRIDE // SLC_BLOCK
    cur = t // SLC_BLOCK
    forced = (lane == 0) | (lane == cur) | (lane == cur - 1)
    score = jnp.where(forced, FORCE_SCORE, imp)
    score = jnp.where(lane * SLC_BLOCK <= t, score, -1.0)
    score = jnp.where(lane < n_slc, score, -2.0)
    cnt = jnp.zeros((tq, LANES), I32)
    for i in range(n_slc):
        ci = score[:, i:i + 1]
        before = (ci > score) | ((ci == score) & (lane > i))
        cnt = cnt + before.astype(I32)
    sel = (cnt < min(SLC_TOPK, n_slc)) & (lane < n_slc)
    sel_ref[...] = sel.astype(F32)


def nsa_cmp_attention(proj, kv_cmp, overlap, B, S, tq):
    M = proj.shape[0]
    nq = S // tq
    G = NSA_KV_HEADS
    gw = NSA_GROUP * HEAD_DIM
    n_blk = kv_cmp.shape[2]
    return pl.pallas_call(
        functools.partial(_cmp_attn_kernel, tq),
        grid=(B, G, nq),
        in_specs=[pl.BlockSpec((tq, gw), lambda b, g, i: (b * nq + i, g)),
                  pl.BlockSpec((None, None, n_blk, HEAD_DIM), lambda b, g, i: (0, b * G + g, 0, 0)),
                  pl.BlockSpec((None, None, n_blk, HEAD_DIM), lambda b, g, i: (1, b * G + g, 0, 0)),
                  pl.BlockSpec(overlap.shape, lambda b, g, i: (0, 0))],
        out_specs=[pl.BlockSpec((tq, gw), lambda b, g, i: (b * nq + i, g)),
                   pl.BlockSpec((None, tq, LANES), lambda b, g, i: (b * G + g, i, 0))],
        out_shape=[jax.ShapeDtypeStruct((M, NSA_HEADS * HEAD_DIM), F32),
                   jax.ShapeDtypeStruct((B * G, S, LANES), F32)],
        compiler_params=_params("parallel", "parallel", "parallel"),
        name="nsa_cmp_attention",
    )(proj, kv_cmp, kv_cmp, overlap)


def _sel_win_kernel(tq, q_ref, gl_ref, bg_ref, oc_ref, sel_ref, ks_ref, vs_ref, kw_ref, vw_ref,
                    cos_ref, sin_ref, ex_ref, o_ref, ks_r, kw_r, vs_b, vw_b):
    qi = pl.program_id(2)
    S = ks_ref.shape[0]
    scale = HEAD_DIM ** -0.5
    span = WINDOW + tq

    @pl.when(qi == 0)
    def _():
        cos = cos_ref[...]
        sin = sin_ref[...]
        ks_r[...] = _rope(ks_ref[...], cos, sin).astype(BF16)
        kw_r[...] = _rope(kw_ref[...], cos, sin).astype(BF16)
        vs_b[...] = vs_ref[...].astype(BF16)
        vw_b[...] = vw_ref[...].astype(BF16)

    q0 = pl.multiple_of(qi * tq, tq)
    cos_q = cos_ref[pl.ds(q0, tq), :]
    sin_q = sin_ref[pl.ds(q0, tq), :]
    sel_keys = _dot(sel_ref[...].astype(BF16), ex_ref[...]) > 0.5
    t_s = q0 + lax.broadcasted_iota(I32, (tq, S), 0)
    k_s = lax.broadcasted_iota(I32, (tq, S), 1)
    mask_s = sel_keys & (k_s <= t_s)
    w0 = pl.multiple_of(jnp.maximum(q0 - WINDOW, 0), tq)
    t_w = q0 + lax.broadcasted_iota(I32, (tq, span), 0)
    k_w = w0 + lax.broadcasted_iota(I32, (tq, span), 1)
    mask_w = (k_w <= t_w) & (t_w - k_w < WINDOW)
    gate = jax.nn.sigmoid(gl_ref[...] + bg_ref[...])
    ks = ks_r[...]
    vs = vs_b[...]
    kw = kw_r[pl.ds(w0, span), :]
    vw = vw_b[pl.ds(w0, span), :]
    for r in range(NSA_GROUP):
        cols = slice(r * HEAD_DIM, (r + 1) * HEAD_DIM)
        q = _rope(q_ref[:, cols], cos_q, sin_q).astype(BF16)
        s = jnp.where(mask_s, _dot_nt(q, ks) * scale, NEG_INF)
        o_s = _dot(_softmax_rows(s).astype(BF16), vs)
        s = jnp.where(mask_w, _dot_nt(q, kw) * scale, NEG_INF)
        o_w = _dot(_softmax_rows(s).astype(BF16), vw)
        o_ref[:, cols] = (gate[:, 3 * r:3 * r + 1] * oc_ref[:, cols]
                          + gate[:, 3 * r + 1:3 * r + 2] * o_s
                          + gate[:, 3 * r + 2:3 * r + 3] * o_w)


def nsa_sel_win_attention(proj, b_gate2, o_cmp, sel, cos, sin, expand, B, S, tq):
    M = proj.shape[0]
    nq = S // tq
    G = NSA_KV_HEADS
    gw = NSA_GROUP * HEAD_DIM

    def kv_spec(cb):
        return pl.BlockSpec((S, HEAD_DIM), lambda b, g, i: (b, cb + g))

    return pl.pallas_call(
        functools.partial(_sel_win_kernel, tq),
        grid=(B, G, nq),
        in_specs=[pl.BlockSpec((tq, gw), lambda b, g, i: (b * nq + i, g)),
                  pl.BlockSpec((tq, LANES), lambda b, g, i: (b * nq + i, CB_GATE + g)),
                  pl.BlockSpec((None, 1, LANES), lambda b, g, i: (g, 0, 0)),
                  pl.BlockSpec((tq, gw), lambda b, g, i: (b * nq + i, g)),
                  pl.BlockSpec((None, tq, LANES), lambda b, g, i: (b * G + g, i, 0)),
                  kv_spec(CB_KS), kv_spec(CB_VS), kv_spec(CB_KW), kv_spec(CB_VW),
                  pl.BlockSpec((S, HEAD_DIM), lambda b, g, i: (0, 0)),
                  pl.BlockSpec((S, HEAD_DIM), lambda b, g, i: (0, 0)),
                  pl.BlockSpec(expand.shape, lambda b, g, i: (0, 0))],
        out_specs=pl.BlockSpec((tq, gw), lambda b, g, i: (b * nq + i, g)),
        out_shape=jax.ShapeDtypeStruct((M, NSA_HEADS * HEAD_DIM), F32),
        scratch_shapes=[pltpu.VMEM((S, HEAD_DIM), BF16)] * 4,
        compiler_params=_params("parallel", "parallel", "arbitrary"),
        name="nsa_sel_win_attention",
    )(proj, proj, b_gate2, o_cmp, sel, proj, proj, proj, proj, cos, sin, expand)


def _retention_kernel(q_ref, k_ref, v_ref, gate_ref, cos_ref, sin_ref, dm_ref, xi_ref, zeta_ref,
                      cdec_ref, gng_ref, gnb_ref, o_ref, state):
    @pl.when(pl.program_id(2) == 0)
    def _():
        state[...] = jnp.zeros_like(state)

    cos = cos_ref[...]
    sin = sin_ref[...]
    q = _rope(q_ref[...], cos, sin)
    k = _rope(k_ref[...], cos, sin) * (HEAD_DIM ** -0.5)
    qb = q.astype(BF16)
    vb = v_ref[...].astype(BF16)
    inner = _dot_nt(qb, k.astype(BF16)) * dm_ref[...]
    r_old = state[...]
    o = _dot(inner.astype(BF16), vb) + _dot(qb, r_old.astype(BF16)) * xi_ref[...]
    state[...] = r_old * cdec_ref[...] + _dot_tn((k * zeta_ref[...]).astype(BF16), vb)
    mu = jnp.mean(o, axis=-1, keepdims=True)
    d = o - mu
    var = jnp.mean(d * d, axis=-1, keepdims=True)
    y = d * lax.rsqrt(var + GN_EPS) * gng_ref[...] + gnb_ref[...]
    gate = gate_ref[...]
    o_ref[...] = gate * jax.nn.sigmoid(gate) * y


def retention(proj, cos, sin, dmask, xi, zeta, cdec, gn_g, gn_b, B, S):
    M = proj.shape[0]
    C = RET_CHUNK
    nc = S // C
    H = RET_HEADS

    def col_spec(cb):
        return pl.BlockSpec((C, HEAD_DIM), lambda b, h, c: (b * nc + c, cb + h))

    def head_spec(rows):
        return pl.BlockSpec((None, rows, HEAD_DIM), lambda b, h, c: (h, 0, 0))

    return pl.pallas_call(
        _retention_kernel,
        grid=(B, H, nc),
        in_specs=[col_spec(CB_RQ), col_spec(CB_RK), col_spec(CB_RV), col_spec(CB_RG),
                  pl.BlockSpec((C, HEAD_DIM), lambda b, h, c: (c, 0)),
                  pl.BlockSpec((C, HEAD_DIM), lambda b, h, c: (c, 0)),
                  head_spec(C), head_spec(C), head_spec(C), head_spec(1), head_spec(1), head_spec(1)],
        out_specs=pl.BlockSpec((C, HEAD_DIM), lambda b, h, c: (b * nc + c, h)),
        out_shape=jax.ShapeDtypeStruct((M, H * HEAD_DIM), F32),
        scratch_shapes=[pltpu.VMEM((HEAD_DIM, HEAD_DIM), F32)],
        compiler_params=_params("parallel", "parallel", "arbitrary"),
        name="retention",
    )(proj, proj, proj, proj, cos, sin, dmask, xi, zeta, cdec, gn_g, gn_b)


def _xattn_kernel(q_ref, k_ref, v_ref, o_ref):
    hd = q_ref.shape[1] // XA_HEADS
    scale = hd ** -0.5
    for h in range(XA_HEADS):
        cols = slice(h * hd, (h + 1) * hd)
        s = _dot_nt(q_ref[:, cols].astype(BF16), k_ref[:, cols].astype(BF16)) * scale
        o_ref[:, cols] = _dot(_softmax_rows(s).astype(BF16), v_ref[:, cols].astype(BF16))


def cross_attention(q, kv, B, S, n_mem, tq):
    M, D = q.shape
    nq = S // tq
    return pl.pallas_call(
        _xattn_kernel,
        grid=(B, nq),
        in_specs=[pl.BlockSpec((tq, D), lambda b, i: (b * nq + i, 0)),
                  pl.BlockSpec((n_mem, D), lambda b, i: (b, 0)),
                  pl.BlockSpec((n_mem, D), lambda b, i: (b, 1))],
        out_specs=pl.BlockSpec((tq, D), lambda b, i: (b * nq + i, 0)),
        out_shape=jax.ShapeDtypeStruct((M, D), F32),
        compiler_params=_params("parallel", "parallel"),
        name="cross_attention",
    )(q, kv, kv)


def _top_k_rows(x, k):
    n = x.shape[0]
    row = lax.broadcasted_iota(I32, x.shape, 0)
    vals, idxs = [], []
    cur = x
    for _ in range(k):
        m = jnp.max(cur, axis=0, keepdims=True)
        idx = jnp.min(jnp.where(cur == m, row, n), axis=0, keepdims=True)
        vals.append(m)
        idxs.append(idx)
        cur = jnp.where(row == idx, -jnp.inf, cur)
    return jnp.concatenate(vals, axis=0), jnp.concatenate(idxs, axis=0)


def _take_rows(table, idx, n):
    out = jnp.zeros(idx.shape, table.dtype)
    for p in range(n):
        out = jnp.where(idx == p, table[p:p + 1, :], out)
    return out


def _peer_topk_kernel(q_ref, keys_ref, a_ref, b_ref, g_ref):
    K = PEER_TOPK
    for h in range(PEER_HEADS):
        halves = []
        for p in range(2):
            c0 = (2 * h + p) * PEER_HALF
            qhp = q_ref[:, c0:c0 + PEER_HALF].astype(BF16)
            s = _dot_nt(keys_ref[2 * h + p], qhp)
            halves.append(_top_k_rows(s, K))
        (s1, i1), (s2, i2) = halves
        cand = jnp.concatenate([s1[p:p + 1, :] + s2 for p in range(K)], axis=0)
        top, pos = _top_k_rows(cand, K)
        a_ref[h * K:(h + 1) * K, :] = _take_rows(i1, pos // K, K)
        b_ref[h * K:(h + 1) * K, :] = _take_rows(i2, pos % K, K)
        e = jnp.exp(top - jnp.max(top, axis=0, keepdims=True))
        g_ref[h * K:(h + 1) * K, :] = e / jnp.sum(e, axis=0, keepdims=True)


def peer_topk(q, keys, tm):
    T, D = q.shape
    HK = PEER_HEADS * PEER_TOPK
    out = jax.ShapeDtypeStruct((HK, T), I32)
    spec = pl.BlockSpec((HK, tm), lambda i: (0, i))
    return pl.pallas_call(
        _peer_topk_kernel,
        grid=(T // tm,),
        in_specs=[pl.BlockSpec((tm, D), lambda i: (i, 0)),
                  pl.BlockSpec(keys.shape, lambda i: (0, 0, 0))],
        out_specs=[spec, spec, spec],
        out_shape=[out, out, jax.ShapeDtypeStruct((HK, T), F32)],
        compiler_params=_params("parallel"),
        name="peer_topk",
    )(q, keys)


def _peer_gate_map_kernel(tg, a_ref, b_ref, g_ref, o_ref):
    row = lax.broadcasted_iota(I32, (PEER_NKEYS, LANES), 0)

    def body(t, carry):
        a = a_ref[pl.ds(t, 1), :]
        b = b_ref[pl.ds(t, 1), :]
        g = g_ref[pl.ds(t, 1), :]
        g_hi = g.astype(BF16).astype(F32)
        g_lo = g - g_hi
        onehot_a = jnp.where(a == row, 1.0, 0.0).astype(BF16)
        hit_b = b == row
        o_ref[t] = (_dot_nt(onehot_a, jnp.where(hit_b, g_hi, 0.0).astype(BF16))
                    + _dot_nt(onehot_a, jnp.where(hit_b, g_lo, 0.0).astype(BF16)))
        return carry

    lax.fori_loop(0, tg, body, 0)


def peer_gate_map(a, b, g, tg):
    T, HK = a.shape
    spec = pl.BlockSpec((tg, HK), lambda i: (i, 0))
    return pl.pallas_call(
        functools.partial(_peer_gate_map_kernel, tg),
        grid=(T // tg,),
        in_specs=[spec, spec, spec],
        out_specs=pl.BlockSpec((tg, PEER_NKEYS, PEER_NKEYS), lambda i: (i, 0, 0)),
        out_shape=jax.ShapeDtypeStruct((T, PEER_NKEYS, PEER_NKEYS), F32),
        compiler_params=_params("parallel"),
        name="peer_gate_map",
    )(a, b, g)


def _peer_ffn_kernel(nr, alpha, x_ref, u_ref, v_ref, gm_ref, lg_ref, lb_ref, o_ref, x_bf, w_bf, acc):
    j = pl.program_id(1)

    @pl.when(j == 0)
    def _():
        x_bf[...] = x_ref[...].astype(BF16)
        acc[...] = jnp.zeros_like(acc)

    h = _dot_nt(x_bf[...], u_ref[...])
    for r in range(nr):
        cols = slice(r * PEER_NKEYS, (r + 1) * PEER_NKEYS)
        w_bf[:, cols] = (gm_ref[:, r, :] * _gelu(h[:, cols])).astype(BF16)
    acc[...] += _dot(w_bf[...], v_ref[...])

    @pl.when(j == pl.num_programs(1) - 1)
    def _():
        o_ref[...] = _layer_norm(alpha * x_ref[...] + acc[...], lg_ref[...], lb_ref[...])


def peer_ffn(x, u, v, gate_map, ln_g, ln_b, alpha, tm, nr):
    T, D = x.shape
    NE = u.shape[0]
    te = nr * PEER_NKEYS
    return pl.pallas_call(
        functools.partial(_peer_ffn_kernel, nr, alpha),
        grid=(T // tm, NE // te),
        in_specs=[pl.BlockSpec((tm, D), lambda i, j: (i, 0)),
                  pl.BlockSpec((te, D), lambda i, j: (j, 0)),
                  pl.BlockSpec((te, D), lambda i, j: (j, 0)),
                  pl.BlockSpec((tm, nr, PEER_NKEYS), lambda i, j: (i, j, 0)),
                  pl.BlockSpec((1, D), lambda i, j: (0, 0)),
                  pl.BlockSpec((1, D), lambda i, j: (0, 0))],
        out_specs=pl.BlockSpec((tm, D), lambda i, j: (i, 0)),
        out_shape=jax.ShapeDtypeStruct((T, D), F32),
        scratch_shapes=[pltpu.VMEM((tm, D), BF16), pltpu.VMEM((tm, te), BF16), pltpu.VMEM((tm, D), F32)],
        compiler_params=_params("parallel", "arbitrary"),
        name="peer_ffn",
    )(x, u, v, gate_map, ln_g.reshape(1, D), ln_b.reshape(1, D))


def _rope_tables(S):
    inv = 1.0 / (ROPE_THETA ** (jnp.arange(0, HEAD_DIM, 2, dtype=F32) / HEAD_DIM))
    ang = jnp.arange(S, dtype=F32)[:, None] * inv[None, :]
    cos = jnp.cos(ang)
    sin = jnp.sin(ang)
    return jnp.concatenate([cos, cos], -1), jnp.concatenate([-sin, sin], -1)


def _retention_tables():
    H, C = RET_HEADS, RET_CHUNK
    log_g = jnp.log(1.0 - 2.0 ** (-5.0 - jnp.arange(H, dtype=F32)))
    i = jnp.arange(C, dtype=F32)
    diff = i[:, None] - i[None, :]
    causal = diff >= 0
    dmask = jnp.where(causal[None], jnp.exp(jnp.where(causal, diff, 0.0)[None] * log_g[:, None, None]), 0.0)
    xi = jnp.exp((i[None, :] + 1.0) * log_g[:, None])
    zeta = jnp.exp((C - 1.0 - i[None, :]) * log_g[:, None])
    cdec = jnp.exp(C * log_g)
    wide = (H, C, HEAD_DIM)
    return (dmask, jnp.broadcast_to(xi[:, :, None], wide), jnp.broadcast_to(zeta[:, :, None], wide),
            jnp.broadcast_to(cdec[:, None, None], (H, 1, HEAD_DIM)))


def _overlap_matrix(n_blk, S):
    n = np.arange(n_blk)[:, None] * CMP_STRIDE
    j = np.arange(LANES)[None, :] * SLC_BLOCK
    n_cmp = (S - CMP_BLOCK) // CMP_STRIDE + 1
    ov = (n < j + SLC_BLOCK) & (n + CMP_BLOCK > j) & (np.arange(n_blk)[:, None] < n_cmp) & (j < S)
    return jnp.asarray(ov, BF16)


def _expand_matrix(S):
    ex = (np.arange(S)[None, :] // SLC_BLOCK) == np.arange(LANES)[:, None]
    return jnp.asarray(ex, BF16)


def _pack_w_in(w_in, b_gate):
    D = w_in.shape[0]
    nsa_w = NSA_HEADS * HEAD_DIM
    kv_w = 6 * NSA_KV_HEADS * HEAD_DIM
    n_gate = 3 * NSA_HEADS
    per_group = n_gate // NSA_KV_HEADS
    gate_w = w_in[:, nsa_w + kv_w:nsa_w + kv_w + n_gate].reshape(D, NSA_KV_HEADS, per_group)
    gate_w = jnp.pad(gate_w, ((0, 0), (0, 0), (0, LANES - per_group))).reshape(D, NSA_KV_HEADS * LANES)
    packed = jnp.concatenate([w_in[:, :nsa_w + kv_w], gate_w, w_in[:, nsa_w + kv_w + n_gate:]], axis=1)
    bg = jnp.pad(b_gate.reshape(NSA_KV_HEADS, 1, per_group), ((0, 0), (0, 0), (0, LANES - per_group)))
    return packed.astype(BF16), bg


def _mixer(x2, B, S, w_in, b_gate, cmp_pos, cmp_w1, cmp_w2, gn_g, gn_b, w_out, ln_g, ln_b, alpha, tables):
    cos, sin, ret_tabs, expand = tables
    G = NSA_KV_HEADS
    w_packed, bg = _pack_w_in(w_in, b_gate)
    proj = matmul(x2, w_packed, tm=1024, tn=768)
    n_blk = S // CMP_STRIDE
    kv = proj[:, CB_KC * LANES:CB_KS * LANES].reshape(B, n_blk, CMP_STRIDE, 2, G, HEAD_DIM)
    x16 = kv.transpose(3, 0, 4, 1, 2, 5).reshape(2, B * G, n_blk, CMP_STRIDE * HEAD_DIM)
    kv_cmp = nsa_compress(x16, cmp_pos.reshape(2, 1, CMP_BLOCK * HEAD_DIM), cmp_w1.astype(BF16),
                          cmp_w2.astype(BF16))
    o_cmp, sel = nsa_cmp_attention(proj, kv_cmp, _overlap_matrix(n_blk, S), B, S, tq=256)
    o_nsa = nsa_sel_win_attention(proj, bg, o_cmp, sel, cos, sin, expand, B, S, tq=128)
    o_ret = retention(proj, cos, sin, *ret_tabs, gn_g.reshape(RET_HEADS, 1, HEAD_DIM),
                      gn_b.reshape(RET_HEADS, 1, HEAD_DIM), B, S)
    return matmul_residual_ln([o_nsa, o_ret], w_out.astype(BF16), x2, ln_g, ln_b, alpha, tm=512)


def _cross(x2, mem2, B, S, wq, wk, wv, wo, ln_g, ln_b, alpha):
    n_mem = mem2.shape[0] // B
    q = matmul(x2, wq.astype(BF16), tm=1024, tn=1024)
    kv = matmul(mem2, jnp.concatenate([wk, wv], axis=1).astype(BF16), tm=mem2.shape[0], tn=1024)
    o = cross_attention(q, kv, B, S, n_mem, tq=256)
    return matmul_residual_ln([o], wo.astype(BF16), x2, ln_g, ln_b, alpha, tm=512)


def _peer(x2, w_q, sub_keys, u_tab, v_tab, ln_g, ln_b, alpha):
    q = matmul(x2, w_q.astype(BF16), tm=1024, tn=1024)
    keys = sub_keys.reshape(PEER_HEADS * 2, PEER_NKEYS, PEER_HALF).astype(BF16)
    a, b, g = peer_topk(q, keys, tm=256)
    gate_map = peer_gate_map(a.T, b.T, g.T, tg=64)
    return peer_ffn(x2, u_tab.astype(BF16), v_tab.astype(BF16), gate_map, ln_g, ln_b, alpha, tm=512, nr=8)


def kernel(x, mem, w_in, b_gate, cmp_pos, cmp_w1, cmp_w2, ret_gn_g, ret_gn_b, w_mix_out, ln1_g, ln1_b,
           xa_wq, xa_wk, xa_wv, xa_wo, ln2_g, ln2_b, peer_wq, peer_sub_keys, peer_u, peer_v, ln3_g, ln3_b):
    B, S, D = x.shape
    depth = w_in.shape[0]
    alpha = (2 * depth) ** 0.25
    tables = (*_rope_tables(S), _retention_tables(), _expand_matrix(S))
    x2 = x.reshape(B * S, D)
    mem2 = mem.reshape(-1, D)
    for l in range(depth):
        x2 = _mixer(x2, B, S, w_in[l], b_gate[l], cmp_pos[l], cmp_w1[l], cmp_w2[l], ret_gn_g[l], ret_gn_b[l],
                    w_mix_out[l], ln1_g[l], ln1_b[l], alpha, tables)
        x2 = _cross(x2, mem2, B, S, xa_wq[l], xa_wk[l], xa_wv[l], xa_wo[l], ln2_g[l], ln2_b[l], alpha)
        x2 = _peer(x2, peer_wq[l], peer_sub_keys[l], peer_u[l], peer_v[l], ln3_g[l], ln3_b[l], alpha)
    return x2.reshape(B, S, D)
```

```python
import functools

import numpy as np
import jax
import jax.numpy as jnp
from jax import lax
from jax.experimental import pallas as pl
from jax.experimental.pallas import tpu as pltpu

F32 = jnp.float32
BF16 = jnp.bfloat16
I32 = jnp.int32

HEAD_DIM = 128
LANES = 128
NSA_HEADS = 8
NSA_KV_HEADS = 2
NSA_GROUP = NSA_HEADS // NSA_KV_HEADS
RET_HEADS = 8
CMP_BLOCK = 32
CMP_STRIDE = 16
SLC_BLOCK = 64
SLC_TOPK = 16
WINDOW = 512
RET_CHUNK = 128
XA_HEADS = 4
PEER_HEADS = 8
PEER_NKEYS = 128
PEER_HALF = 128
PEER_TOPK = 16
ROPE_THETA = 10000.0
LN_EPS = 1e-5
GN_EPS = 1e-5
NEG_INF = -1e30
FORCE_SCORE = 1e9
VMEM_LIMIT = 56 * 1024 * 1024

CB_Q = 0
CB_KC, CB_VC, CB_KS, CB_VS, CB_KW, CB_VW = 8, 10, 12, 14, 16, 18
CB_GATE = 20
CB_RQ, CB_RK, CB_RV, CB_RG = 22, 30, 38, 46
P_PACKED = 54 * LANES


def _dot(a, b):
    return jnp.dot(a, b, preferred_element_type=F32)


def _dot_nt(a, b):
    return lax.dot_general(a, b, (((1,), (1,)), ((), ())), preferred_element_type=F32)


def _dot_tn(a, b):
    return lax.dot_general(a, b, (((0,), (0,)), ((), ())), preferred_element_type=F32)


def _gelu(x):
    return 0.5 * x * (1.0 + lax.erf(x * (2.0 ** -0.5)))


def _rope(x, cos, sin_signed):
    return x * cos + pltpu.roll(x, HEAD_DIM // 2, axis=1) * sin_signed


def _layer_norm(y, g, b):
    mu = jnp.mean(y, axis=-1, keepdims=True)
    d = y - mu
    var = jnp.mean(d * d, axis=-1, keepdims=True)
    return d * lax.rsqrt(var + LN_EPS) * g + b


def _softmax_rows(s):
    m = jnp.max(s, axis=-1, keepdims=True)
    e = jnp.exp(s - m)
    return e / jnp.sum(e, axis=-1, keepdims=True)


def _params(*sem):
    return pltpu.CompilerParams(dimension_semantics=sem, vmem_limit_bytes=VMEM_LIMIT)


def _mm_kernel(a_ref, w_ref, o_ref, a_bf):
    @pl.when(pl.program_id(1) == 0)
    def _():
        a_bf[...] = a_ref[...].astype(BF16)

    o_ref[...] = _dot(a_bf[...], w_ref[...]).astype(o_ref.dtype)


def matmul(a, w, tm, tn):
    M, K = a.shape
    N = w.shape[1]
    assert M % tm == 0 and N % tn == 0
    return pl.pallas_call(
        _mm_kernel,
        grid=(M // tm, N // tn),
        in_specs=[pl.BlockSpec((tm, K), lambda i, j: (i, 0)),
                  pl.BlockSpec((K, tn), lambda i, j: (0, j))],
        out_specs=pl.BlockSpec((tm, tn), lambda i, j: (i, j)),
        out_shape=jax.ShapeDtypeStruct((M, N), F32),
        scratch_shapes=[pltpu.VMEM((tm, K), BF16)],
        compiler_params=_params("parallel", "arbitrary"),
        name="matmul",
    )(a, w)


def _mm_ln_kernel(n_in, alpha, *refs):
    a_refs = refs[:n_in]
    w_ref, x_ref, g_ref, b_ref, o_ref = refs[n_in:]
    acc = None
    off = 0
    for a_ref in a_refs:
        k = a_ref.shape[1]
        part = _dot(a_ref[...].astype(BF16), w_ref[off:off + k, :])
        acc = part if acc is None else acc + part
        off += k
    o_ref[...] = _layer_norm(alpha * x_ref[...] + acc, g_ref[...], b_ref[...])


def matmul_residual_ln(a_list, w, x, g, b, alpha, tm):
    M, D = x.shape
    n_in = len(a_list)
    in_specs = [pl.BlockSpec((tm, a.shape[1]), lambda i: (i, 0)) for a in a_list]
    in_specs += [pl.BlockSpec(w.shape, lambda i: (0, 0)),
                 pl.BlockSpec((tm, D), lambda i: (i, 0)),
                 pl.BlockSpec((1, D), lambda i: (0, 0)),
                 pl.BlockSpec((1, D), lambda i: (0, 0))]
    return pl.pallas_call(
        functools.partial(_mm_ln_kernel, n_in, alpha),
        grid=(M // tm,),
        in_specs=in_specs,
        out_specs=pl.BlockSpec((tm, D), lambda i: (i, 0)),
        out_shape=jax.ShapeDtypeStruct((M, D), F32),
        compiler_params=_params("parallel"),
        name="matmul_residual_ln",
    )(*a_list, w, x, g.reshape(1, D), b.reshape(1, D))


def _compress_kernel(x_ref, pos_ref, w1_ref, w2_ref, o_ref):
    x = x_ref[...]
    half = x.shape[1]
    pos = pos_ref[...]
    lo = (x + pos[:, :half]).astype(BF16)
    hi = (x + pos[:, half:]).astype(BF16)
    p_lo = _dot(lo, w1_ref[:half, :])
    p_hi = _dot(hi, w1_ref[half:, :])
    h = p_lo + pltpu.roll(p_hi, x.shape[0] - 1, axis=0)
    o_ref[...] = _dot(_gelu(h).astype(BF16), w2_ref[...])


def nsa_compress(x16, pos, w1, w2):
    _, BG, n_blk, wd = x16.shape
    return pl.pallas_call(
        _compress_kernel,
        grid=(2, BG),
        in_specs=[pl.BlockSpec((None, None, n_blk, wd), lambda i, j: (i, j, 0, 0)),
                  pl.BlockSpec((None, 1, 2 * wd), lambda i, j: (i, 0, 0)),
                  pl.BlockSpec((None, 2 * wd, HEAD_DIM), lambda i, j: (i, 0, 0)),
                  pl.BlockSpec((None, HEAD_DIM, HEAD_DIM), lambda i, j: (i, 0, 0))],
        out_specs=pl.BlockSpec((None, None, n_blk, HEAD_DIM), lambda i, j: (i, j, 0, 0)),
        out_shape=jax.ShapeDtypeStruct((2, BG, n_blk, HEAD_DIM), F32),
        compiler_params=_params("parallel", "parallel"),
        name="nsa_compress",
    )(x16, pos, w1, w2)


def _cmp_attn_kernel(tq, q_ref, kc_ref, vc_ref, ov_ref, o_ref, sel_ref):
    qi = pl.program_id(2)
    scale = HEAD_DIM ** -0.5
    t = qi * tq + lax.broadcasted_iota(I32, (tq, LANES), 0)
    lane = lax.broadcasted_iota(I32, (tq, LANES), 1)
    vis = (lane * CMP_STRIDE + (CMP_BLOCK - 1)) <= t
    kc = kc_ref[...].astype(BF16)
    vc = vc_ref[...].astype(BF16)
    psum = jnp.zeros((tq, LANES), F32)
    for r in range(NSA_GROUP):
        q = q_ref[:, r * HEAD_DIM:(r + 1) * HEAD_DIM].astype(BF16)
        s = _dot_nt(q, kc) * scale
        s = jnp.where(vis, s, NEG_INF)
        m = jnp.max(s, axis=-1, keepdims=True)
        e = jnp.where(vis, jnp.exp(s - m), 0.0)
        l = jnp.sum(e, axis=-1, keepdims=True)
        p = e / jnp.where(l > 0.0, l, 1.0)
        o_ref[:, r * HEAD_DIM:(r + 1) * HEAD_DIM] = _dot(p.astype(BF16), vc)
        psum = psum + p
    p_hi = psum.astype(BF16)
    p_lo = (psum - p_hi.astype(F32)).astype(BF16)
    imp = _dot(p_hi, ov_ref[...]) + _dot(p_lo, ov_ref[...])
    n_slc = ov_ref.shape[0] * CMP_STRIDE // SLC_BLOCK
    cur = t // SLC_BLOCK
    forced = (lane == 0) | (lane == cur) | (lane == cur - 1)
    score = jnp.where(forced, FORCE_SCORE, imp)
    score = jnp.where(lane * SLC_BLOCK <= t, score, -1.0)
    score = jnp.where(lane < n_slc, score, -2.0)
    cnt = jnp.zeros((tq, LANES), I32)
    for i in range(n_slc):
        ci = score[:, i:i + 1]
        before = (ci > score) | ((ci == score) & (lane > i))
        cnt = cnt + before.astype(I32)
    sel = (cnt < min(SLC_TOPK, n_slc)) & (lane < n_slc)
    sel_ref[...] = sel.astype(F32)


def nsa_cmp_attention(proj, kv_cmp, overlap, B, S, tq):
    M = proj.shape[0]
    nq = S // tq
    G = NSA_KV_HEADS
    gw = NSA_GROUP * HEAD_DIM
    n_blk = kv_cmp.shape[2]
    return pl.pallas_call(
        functools.partial(_cmp_attn_kernel, tq),
        grid=(B, G, nq),
        in_specs=[pl.BlockSpec((tq, gw), lambda b, g, i: (b * nq + i, g)),
                  pl.BlockSpec((None, None, n_blk, HEAD_DIM), lambda b, g, i: (0, b * G + g, 0, 0)),
                  pl.BlockSpec((None, None, n_blk, HEAD_DIM), lambda b, g, i: (1, b * G + g, 0, 0)),
                  pl.BlockSpec(overlap.shape, lambda b, g, i: (0, 0))],
        out_specs=[pl.BlockSpec((tq, gw), lambda b, g, i: (b * nq + i, g)),
                   pl.BlockSpec((None, tq, LANES), lambda b, g, i: (b * G + g, i, 0))],
        out_shape=[jax.ShapeDtypeStruct((M, NSA_HEADS * HEAD_DIM), F32),
                   jax.ShapeDtypeStruct((B * G, S, LANES), F32)],
        compiler_params=_params("parallel", "parallel", "parallel"),
        name="nsa_cmp_attention",
    )(proj, kv_cmp, kv_cmp, overlap)


def _sel_win_kernel(tq, q_ref, gl_ref, bg_ref, oc_ref, sel_ref, ks_ref, vs_ref, kw_ref, vw_ref,
                    cos_ref, sin_ref, ex_ref, o_ref, ks_r, kw_r, vs_b, vw_b):
    qi = pl.program_id(2)
    S = ks_ref.shape[0]
    scale = HEAD_DIM ** -0.5
    span = WINDOW + tq

    @pl.when(qi == 0)
    def _():
        cos = cos_ref[...]
        sin = sin_ref[...]
        ks_r[...] = _rope(ks_ref[...], cos, sin).astype(BF16)
        kw_r[...] = _rope(kw_ref[...], cos, sin).astype(BF16)
        vs_b[...] = vs_ref[...].astype(BF16)
        vw_b[...] = vw_ref[...].astype(BF16)

    q0 = pl.multiple_of(qi * tq, tq)
    cos_q = cos_ref[pl.ds(q0, tq), :]
    sin_q = sin_ref[pl.ds(q0, tq), :]
    sel_keys = _dot(sel_ref[...].astype(BF16), ex_ref[...]) > 0.5
    t_s = q0 + lax.broadcasted_iota(I32, (tq, S), 0)
    k_s = lax.broadcasted_iota(I32, (tq, S), 1)
    mask_s = sel_keys & (k_s <= t_s)
    w0 = pl.multiple_of(jnp.maximum(q0 - WINDOW, 0), tq)
    t_w = q0 + lax.broadcasted_iota(I32, (tq, span), 0)
    k_w = w0 + lax.broadcasted_iota(I32, (tq, span), 1)
    mask_w = (k_w <= t_w) & (t_w - k_w < WINDOW)
    gate = jax.nn.sigmoid(gl_ref[...] + bg_ref[...])
    ks = ks_r[...]
    vs = vs_b[...]
    kw = kw_r[pl.ds(w0, span), :]
    vw = vw_b[pl.ds(w0, span), :]
    for r in range(NSA_GROUP):
        cols = slice(r * HEAD_DIM, (r + 1) * HEAD_DIM)
        q = _rope(q_ref[:, cols], cos_q, sin_q).astype(BF16)
        s = jnp.where(mask_s, _dot_nt(q, ks) * scale, NEG_INF)
        o_s = _dot(_softmax_rows(s).astype(BF16), vs)
        s = jnp.where(mask_w, _dot_nt(q, kw) * scale, NEG_INF)
        o_w = _dot(_softmax_rows(s).astype(BF16), vw)
        o_ref[:, cols] = (gate[:, 3 * r:3 * r + 1] * oc_ref[:, cols]
                          + gate[:, 3 * r + 1:3 * r + 2] * o_s
                          + gate[:, 3 * r + 2:3 * r + 3] * o_w)


def nsa_sel_win_attention(proj, b_gate2, o_cmp, sel, cos, sin, expand, B, S, tq):
    M = proj.shape[0]
    nq = S // tq
    G = NSA_KV_HEADS
    gw = NSA_GROUP * HEAD_DIM

    def kv_spec(cb):
        return pl.BlockSpec((S, HEAD_DIM), lambda b, g, i: (b, cb + g))

    return pl.pallas_call(
        functools.partial(_sel_win_kernel, tq),
        grid=(B, G, nq),
        in_specs=[pl.BlockSpec((tq, gw), lambda b, g, i: (b * nq + i, g)),
                  pl.BlockSpec((tq, LANES), lambda b, g, i: (b * nq + i, CB_GATE + g)),
                  pl.BlockSpec((None, 1, LANES), lambda b, g, i: (g, 0, 0)),
                  pl.BlockSpec((tq, gw), lambda b, g, i: (b * nq + i, g)),
                  pl.BlockSpec((None, tq, LANES), lambda b, g, i: (b * G + g, i, 0)),
                  kv_spec(CB_KS), kv_spec(CB_VS), kv_spec(CB_KW), kv_spec(CB_VW),
                  pl.BlockSpec((S, HEAD_DIM), lambda b, g, i: (0, 0)),
                  pl.BlockSpec((S, HEAD_DIM), lambda b, g, i: (0, 0)),
                  pl.BlockSpec(expand.shape, lambda b, g, i: (0, 0))],
        out_specs=pl.BlockSpec((tq, gw), lambda b, g, i: (b * nq + i, g)),
        out_shape=jax.ShapeDtypeStruct((M, NSA_HEADS * HEAD_DIM), F32),
        scratch_shapes=[pltpu.VMEM((S, HEAD_DIM), BF16)] * 4,
        compiler_params=_params("parallel", "parallel", "arbitrary"),
        name="nsa_sel_win_attention",
    )(proj, proj, b_gate2, o_cmp, sel, proj, proj, proj, proj, cos, sin, expand)


def _retention_kernel(q_ref, k_ref, v_ref, gate_ref, cos_ref, sin_ref, dm_ref, xi_ref, zeta_ref,
                      cdec_ref, gng_ref, gnb_ref, o_ref, state):
    @pl.when(pl.program_id(2) == 0)
    def _():
        state[...] = jnp.zeros_like(state)

    cos = cos_ref[...]
    sin = sin_ref[...]
    q = _rope(q_ref[...], cos, sin)
    k = _rope(k_ref[...], cos, sin) * (HEAD_DIM ** -0.5)
    qb = q.astype(BF16)
    vb = v_ref[...].astype(BF16)
    inner = _dot_nt(qb, k.astype(BF16)) * dm_ref[...]
    r_old = state[...]
    o = _dot(inner.astype(BF16), vb) + _dot(qb, r_old.astype(BF16)) * xi_ref[...]
    state[...] = r_old * cdec_ref[...] + _dot_tn((k * zeta_ref[...]).astype(BF16), vb)
    mu = jnp.mean(o, axis=-1, keepdims=True)
    d = o - mu
    var = jnp.mean(d * d, axis=-1, keepdims=True)
    y = d * lax.rsqrt(var + GN_EPS) * gng_ref[...] + gnb_ref[...]
    gate = gate_ref[...]
    o_ref[...] = gate * jax.nn.sigmoid(gate) * y


def retention(proj, cos, sin, dmask, xi, zeta, cdec, gn_g, gn_b, B, S):
    M = proj.shape[0]
    C = RET_CHUNK
    nc = S // C
    H = RET_HEADS

    def col_spec(cb):
        return pl.BlockSpec((C, HEAD_DIM), lambda b, h, c: (b * nc + c, cb + h))

    def head_spec(rows):
        return pl.BlockSpec((None, rows, HEAD_DIM), lambda b, h, c: (h, 0, 0))

    return pl.pallas_call(
        _retention_kernel,
        grid=(B, H, nc),
        in_specs=[col_spec(CB_RQ), col_spec(CB_RK), col_spec(CB_RV), col_spec(CB_RG),
                  pl.BlockSpec((C, HEAD_DIM), lambda b, h, c: (c, 0)),
                  pl.BlockSpec((C, HEAD_DIM), lambda b, h, c: (c, 0)),
                  head_spec(C), head_spec(C), head_spec(C), head_spec(1), head_spec(1), head_spec(1)],
        out_specs=pl.BlockSpec((C, HEAD_DIM), lambda b, h, c: (b * nc + c, h)),
        out_shape=jax.ShapeDtypeStruct((M, H * HEAD_DIM), F32),
        scratch_shapes=[pltpu.VMEM((HEAD_DIM, HEAD_DIM), F32)],
        compiler_params=_params("parallel", "parallel", "arbitrary"),
        name="retention",
    )(proj, proj, proj, proj, cos, sin, dmask, xi, zeta, cdec, gn_g, gn_b)


def _xattn_kernel(q_ref, k_ref, v_ref, o_ref):
    hd = q_ref.shape[1] // XA_HEADS
    scale = hd ** -0.5
    for h in range(XA_HEADS):
        cols = slice(h * hd, (h + 1) * hd)
        s = _dot_nt(q_ref[:, cols].astype(BF16), k_ref[:, cols].astype(BF16)) * scale
        o_ref[:, cols] = _dot(_softmax_rows(s).astype(BF16), v_ref[:, cols].astype(BF16))


def cross_attention(q, kv, B, S, n_mem, tq):
    M, D = q.shape
    nq = S // tq
    return pl.pallas_call(
        _xattn_kernel,
        grid=(B, nq),
        in_specs=[pl.BlockSpec((tq, D), lambda b, i: (b * nq + i, 0)),
                  pl.BlockSpec((n_mem, D), lambda b, i: (b, 0)),
                  pl.BlockSpec((n_mem, D), lambda b, i: (b, 1))],
        out_specs=pl.BlockSpec((tq, D), lambda b, i: (b * nq + i, 0)),
        out_shape=jax.ShapeDtypeStruct((M, D), F32),
        compiler_params=_params("parallel", "parallel"),
        name="cross_attention",
    )(q, kv, kv)


def _top_k_rows(x, k):
    n = x.shape[0]
    row = lax.broadcasted_iota(I32, x.shape, 0)
    vals, idxs = [], []
    cur = x
    for _ in range(k):
        m = jnp.max(cur, axis=0, keepdims=True)
        idx = jnp.min(jnp.where(cur == m, row, n), axis=0, keepdims=True)
        vals.append(m)
        idxs.append(idx)
        cur = jnp.where(row == idx, -jnp.inf, cur)
    return jnp.concatenate(vals, axis=0), jnp.concatenate(idxs, axis=0)


def _take_rows(table, idx, n):
    out = jnp.zeros(idx.shape, table.dtype)
    for p in range(n):
        out = jnp.where(idx == p, table[p:p + 1, :], out)
    return out


def _peer_topk_kernel(q_ref, keys_ref, a_ref, b_ref, g_ref):
    K = PEER_TOPK
    for h in range(PEER_HEADS):
        halves = []
        for p in range(2):
            c0 = (2 * h + p) * PEER_HALF
            qhp = q_ref[:, c0:c0 + PEER_HALF].astype(BF16)
            s = _dot_nt(keys_ref[2 * h + p], qhp)
            halves.append(_top_k_rows(s, K))
        (s1, i1), (s2, i2) = halves
        cand = jnp.concatenate([s1[p:p + 1, :] + s2 for p in range(K)], axis=0)
        top, pos = _top_k_rows(cand, K)
        a_ref[h * K:(h + 1) * K, :] = _take_rows(i1, pos // K, K)
        b_ref[h * K:(h + 1) * K, :] = _take_rows(i2, pos % K, K)
        e = jnp.exp(top - jnp.max(top, axis=0, keepdims=True))
        g_ref[h * K:(h + 1) * K, :] = e / jnp.sum(e, axis=0, keepdims=True)


def peer_topk(q, keys, tm):
    T, D = q.shape
    HK = PEER_HEADS * PEER_TOPK
    out = jax.ShapeDtypeStruct((HK, T), I32)
    spec = pl.BlockSpec((HK, tm), lambda i: (0, i))
    return pl.pallas_call(
        _peer_topk_kernel,
        grid=(T // tm,),
        in_specs=[pl.BlockSpec((tm, D), lambda i: (i, 0)),
                  pl.BlockSpec(keys.shape, lambda i: (0, 0, 0))],
        out_specs=[spec, spec, spec],
        out_shape=[out, out, jax.ShapeDtypeStruct((HK, T), F32)],
        compiler_params=_params("parallel"),
        name="peer_topk",
    )(q, keys)


GATE_MAP_UNROLL = 8


def _peer_gate_map_kernel(tg, a_ref, b_ref, g_ref, o_ref):
    row = lax.broadcasted_iota(I32, (PEER_NKEYS, LANES), 0)

    def body(i, carry):
        t0 = pl.multiple_of(i * GATE_MAP_UNROLL, GATE_MAP_UNROLL)
        a8 = a_ref[pl.ds(t0, GATE_MAP_UNROLL), :]
        b8 = b_ref[pl.ds(t0, GATE_MAP_UNROLL), :]
        g8 = g_ref[pl.ds(t0, GATE_MAP_UNROLL), :]
        g8_hi = g8.astype(BF16).astype(F32)
        g8_lo = g8 - g8_hi
        for u in range(GATE_MAP_UNROLL):
            onehot_a = jnp.where(a8[u:u + 1, :] == row, 1.0, 0.0).astype(BF16)
            hit_b = b8[u:u + 1, :] == row
            b_hi = jnp.where(hit_b, g8_hi[u:u + 1, :], 0.0).astype(BF16)
            b_lo = jnp.where(hit_b, g8_lo[u:u + 1, :], 0.0).astype(BF16)
            gm = _dot_nt(jnp.concatenate([onehot_a, onehot_a], axis=1), jnp.concatenate([b_hi, b_lo], axis=1))
            o_ref[pl.ds(t0 + u, PEER_NKEYS, stride=tg), :] = gm
        return carry

    lax.fori_loop(0, tg // GATE_MAP_UNROLL, body, 0)


def peer_gate_map(a, b, g, tg):
    T, HK = a.shape
    spec = pl.BlockSpec((tg, HK), lambda i: (i, 0))
    out = pl.pallas_call(
        functools.partial(_peer_gate_map_kernel, tg),
        grid=(T // tg,),
        in_specs=[spec, spec, spec],
        out_specs=pl.BlockSpec((PEER_NKEYS * tg, PEER_NKEYS), lambda i: (i, 0)),
        out_shape=jax.ShapeDtypeStruct((T * PEER_NKEYS, PEER_NKEYS), F32),
        compiler_params=_params("parallel"),
        name="peer_gate_map",
    )(a, b, g)
    return out.reshape(T // tg, PEER_NKEYS, tg, PEER_NKEYS)


def _peer_ffn_kernel(alpha, x_ref, u_ref, v_ref, gm_ref, lg_ref, lb_ref, o_ref, x_bf, h_scr, w_bf, acc):
    j = pl.program_id(1)
    n_tiles = pl.num_programs(1) - 1
    n_sub, nr, tg, _ = gm_ref.shape
    slot = j % 2

    def project(dst):
        h_scr[dst] = _dot_nt(x_bf[...], u_ref[...])

    def gate_and_accumulate(src):
        for r in range(nr):
            cols = slice(r * PEER_NKEYS, (r + 1) * PEER_NKEYS)
            for s in range(n_sub):
                rows = slice(s * tg, (s + 1) * tg)
                w_bf[rows, cols] = (gm_ref[s, r] * _gelu(h_scr[src, rows, cols])).astype(BF16)
        acc[...] += _dot(w_bf[...], v_ref[...])

    @pl.when(j == 0)
    def _():
        x_bf[...] = x_ref[...].astype(BF16)
        acc[...] = jnp.zeros_like(acc)
        project(0)

    @pl.when((j > 0) & (j < n_tiles))
    def _():
        project(slot)
        gate_and_accumulate(1 - slot)

    @pl.when(j == n_tiles)
    def _():
        gate_and_accumulate(1 - slot)
        o_ref[...] = _layer_norm(alpha * x_ref[...] + acc[...], lg_ref[...], lb_ref[...])


def peer_ffn(x, u, v, gate_map, ln_g, ln_b, alpha, tm, nr):
    T, D = x.shape
    NE = u.shape[0]
    te = nr * PEER_NKEYS
    tg = gate_map.shape[2]
    n_tiles = NE // te
    return pl.pallas_call(
        functools.partial(_peer_ffn_kernel, alpha),
        grid=(T // tm, n_tiles + 1),
        in_specs=[pl.BlockSpec((tm, D), lambda i, j: (i, 0)),
                  pl.BlockSpec((te, D), lambda i, j: (jnp.minimum(j, n_tiles - 1), 0)),
                  pl.BlockSpec((te, D), lambda i, j: (jnp.maximum(j - 1, 0), 0)),
                  pl.BlockSpec((tm // tg, nr, tg, PEER_NKEYS), lambda i, j: (i, jnp.maximum(j - 1, 0), 0, 0)),
                  pl.BlockSpec((1, D), lambda i, j: (0, 0)),
                  pl.BlockSpec((1, D), lambda i, j: (0, 0))],
        out_specs=pl.BlockSpec((tm, D), lambda i, j: (i, 0)),
        out_shape=jax.ShapeDtypeStruct((T, D), F32),
        scratch_shapes=[pltpu.VMEM((tm, D), BF16), pltpu.VMEM((2, tm, te), F32), pltpu.VMEM((tm, te), BF16),
                        pltpu.VMEM((tm, D), F32)],
        compiler_params=_params("parallel", "arbitrary"),
        name="peer_ffn",
    )(x, u, v, gate_map, ln_g.reshape(1, D), ln_b.reshape(1, D))


def _rope_tables(S):
    inv = 1.0 / (ROPE_THETA ** (jnp.arange(0, HEAD_DIM, 2, dtype=F32) / HEAD_DIM))
    ang = jnp.arange(S, dtype=F32)[:, None] * inv[None, :]
    cos = jnp.cos(ang)
    sin = jnp.sin(ang)
    return jnp.concatenate([cos, cos], -1), jnp.concatenate([-sin, sin], -1)


def _retention_tables():
    H, C = RET_HEADS, RET_CHUNK
    log_g = jnp.log(1.0 - 2.0 ** (-5.0 - jnp.arange(H, dtype=F32)))
    i = jnp.arange(C, dtype=F32)
    diff = i[:, None] - i[None, :]
    causal = diff >= 0
    dmask = jnp.where(causal[None], jnp.exp(jnp.where(causal, diff, 0.0)[None] * log_g[:, None, None]), 0.0)
    xi = jnp.exp((i[None, :] + 1.0) * log_g[:, None])
    zeta = jnp.exp((C - 1.0 - i[None, :]) * log_g[:, None])
    cdec = jnp.exp(C * log_g)
    wide = (H, C, HEAD_DIM)
    return (dmask, jnp.broadcast_to(xi[:, :, None], wide), jnp.broadcast_to(zeta[:, :, None], wide),
            jnp.broadcast_to(cdec[:, None, None], (H, 1, HEAD_DIM)))


def _overlap_matrix(n_blk, S):
    n = np.arange(n_blk)[:, None] * CMP_STRIDE
    j = np.arange(LANES)[None, :] * SLC_BLOCK
    n_cmp = (S - CMP_BLOCK) // CMP_STRIDE + 1
    ov = (n < j + SLC_BLOCK) & (n + CMP_BLOCK > j) & (np.arange(n_blk)[:, None] < n_cmp) & (j < S)
    return jnp.asarray(ov, BF16)


def _expand_matrix(S):
    ex = (np.arange(S)[None, :] // SLC_BLOCK) == np.arange(LANES)[:, None]
    return jnp.asarray(ex, BF16)


def _pack_w_in(w_in, b_gate):
    D = w_in.shape[0]
    nsa_w = NSA_HEADS * HEAD_DIM
    kv_w = 6 * NSA_KV_HEADS * HEAD_DIM
    n_gate = 3 * NSA_HEADS
    per_group = n_gate // NSA_KV_HEADS
    gate_w = w_in[:, nsa_w + kv_w:nsa_w + kv_w + n_gate].reshape(D, NSA_KV_HEADS, per_group)
    gate_w = jnp.pad(gate_w, ((0, 0), (0, 0), (0, LANES - per_group))).reshape(D, NSA_KV_HEADS * LANES)
    packed = jnp.concatenate([w_in[:, :nsa_w + kv_w], gate_w, w_in[:, nsa_w + kv_w + n_gate:]], axis=1)
    bg = jnp.pad(b_gate.reshape(NSA_KV_HEADS, 1, per_group), ((0, 0), (0, 0), (0, LANES - per_group)))
    return packed.astype(BF16), bg


def _mixer(x2, B, S, w_in, b_gate, cmp_pos, cmp_w1, cmp_w2, gn_g, gn_b, w_out, ln_g, ln_b, alpha, tables):
    cos, sin, ret_tabs, expand = tables
    G = NSA_KV_HEADS
    w_packed, bg = _pack_w_in(w_in, b_gate)
    proj = matmul(x2, w_packed, tm=1024, tn=768)
    n_blk = S // CMP_STRIDE
    kv = proj[:, CB_KC * LANES:CB_KS * LANES].reshape(B, n_blk, CMP_STRIDE, 2, G, HEAD_DIM)
    x16 = kv.transpose(3, 0, 4, 1, 2, 5).reshape(2, B * G, n_blk, CMP_STRIDE * HEAD_DIM)
    kv_cmp = nsa_compress(x16, cmp_pos.reshape(2, 1, CMP_BLOCK * HEAD_DIM), cmp_w1.astype(BF16),
                          cmp_w2.astype(BF16))
    o_cmp, sel = nsa_cmp_attention(proj, kv_cmp, _overlap_matrix(n_blk, S), B, S, tq=256)
    o_nsa = nsa_sel_win_attention(proj, bg, o_cmp, sel, cos, sin, expand, B, S, tq=128)
    o_ret = retention(proj, cos, sin, *ret_tabs, gn_g.reshape(RET_HEADS, 1, HEAD_DIM),
                      gn_b.reshape(RET_HEADS, 1, HEAD_DIM), B, S)
    return matmul_residual_ln([o_nsa, o_ret], w_out.astype(BF16), x2, ln_g, ln_b, alpha, tm=512)


def _cross(x2, mem2, B, S, wq, wk, wv, wo, ln_g, ln_b, alpha):
    n_mem = mem2.shape[0] // B
    q = matmul(x2, wq.astype(BF16), tm=1024, tn=1024)
    kv = matmul(mem2, jnp.concatenate([wk, wv], axis=1).astype(BF16), tm=mem2.shape[0], tn=1024)
    o = cross_attention(q, kv, B, S, n_mem, tq=256)
    return matmul_residual_ln([o], wo.astype(BF16), x2, ln_g, ln_b, alpha, tm=512)


def _peer(x2, w_q, sub_keys, u_tab, v_tab, ln_g, ln_b, alpha):
    q = matmul(x2, w_q.astype(BF16), tm=1024, tn=1024)
    keys = sub_keys.reshape(PEER_HEADS * 2, PEER_NKEYS, PEER_HALF).astype(BF16)
    a, b, g = peer_topk(q, keys, tm=256)
    gate_map = peer_gate_map(a.T, b.T, g.T, tg=128)
    return peer_ffn(x2, u_tab.astype(BF16), v_tab.astype(BF16), gate_map, ln_g, ln_b, alpha, tm=512, nr=8)


def kernel(x, mem, w_in, b_gate, cmp_pos, cmp_w1, cmp_w2, ret_gn_g, ret_gn_b, w_mix_out, ln1_g, ln1_b,
           xa_wq, xa_wk, xa_wv, xa_wo, ln2_g, ln2_b, peer_wq, peer_sub_keys, peer_u, peer_v, ln3_g, ln3_b):
    B, S, D = x.shape
    depth = w_in.shape[0]
    alpha = (2 * depth) ** 0.25
    tables = (*_rope_tables(S), _retention_tables(), _expand_matrix(S))
    x2 = x.reshape(B * S, D)
    mem2 = mem.reshape(-1, D)
    for l in range(depth):
        x2 = _mixer(x2, B, S, w_in[l], b_gate[l], cmp_pos[l], cmp_w1[l], cmp_w2[l], ret_gn_g[l], ret_gn_b[l],
                    w_mix_out[l], ln1_g[l], ln1_b[l], alpha, tables)
        x2 = _cross(x2, mem2, B, S, xa_wq[l], xa_wk[l], xa_wv[l], xa_wo[l], ln2_g[l], ln2_b[l], alpha)
        x2 = _peer(x2, peer_wq[l], peer_sub_keys[l], peer_u[l], peer_v[l], ln3_g[l], ln3_b[l], alpha)
    return x2.reshape(B, S, D)
```

```python
import functools

import numpy as np
import jax
import jax.numpy as jnp
from jax import lax
from jax.experimental import pallas as pl
from jax.experimental.pallas import tpu as pltpu

F32 = jnp.float32
BF16 = jnp.bfloat16
I32 = jnp.int32

HEAD_DIM = 128
LANES = 128
NSA_HEADS = 8
NSA_KV_HEADS = 2
NSA_GROUP = NSA_HEADS // NSA_KV_HEADS
RET_HEADS = 8
CMP_BLOCK = 32
CMP_STRIDE = 16
SLC_BLOCK = 64
SLC_TOPK = 16
WINDOW = 512
RET_CHUNK = 128
XA_HEADS = 4
PEER_HEADS = 8
PEER_NKEYS = 128
PEER_HALF = 128
PEER_TOPK = 16
ROPE_THETA = 10000.0
LN_EPS = 1e-5
GN_EPS = 1e-5
NEG_INF = -1e30
FORCE_SCORE = 1e9
VMEM_LIMIT = 56 * 1024 * 1024

CB_Q = 0
CB_KC, CB_VC, CB_KS, CB_VS, CB_KW, CB_VW = 8, 10, 12, 14, 16, 18
CB_GATE = 20
CB_PAD = 22
CB_RQ, CB_RK, CB_RV, CB_RG = 24, 32, 40, 48
P_PACKED = 56 * LANES


def _dot(a, b):
    return jnp.dot(a, b, preferred_element_type=F32)


def _dot_nt(a, b):
    return lax.dot_general(a, b, (((1,), (1,)), ((), ())), preferred_element_type=F32)


def _dot_tn(a, b):
    return lax.dot_general(a, b, (((0,), (0,)), ((), ())), preferred_element_type=F32)


def _gelu(x):
    return 0.5 * x * (1.0 + lax.erf(x * (2.0 ** -0.5)))


def _rope(x, cos, sin_signed):
    return x * cos + pltpu.roll(x, HEAD_DIM // 2, axis=1) * sin_signed


def _layer_norm(y, g, b):
    mu = jnp.mean(y, axis=-1, keepdims=True)
    d = y - mu
    var = jnp.mean(d * d, axis=-1, keepdims=True)
    return d * lax.rsqrt(var + LN_EPS) * g + b


def _softmax_rows(s):
    m = jnp.max(s, axis=-1, keepdims=True)
    e = jnp.exp(s - m)
    return e / jnp.sum(e, axis=-1, keepdims=True)


def _params(*sem):
    return pltpu.CompilerParams(dimension_semantics=sem, vmem_limit_bytes=VMEM_LIMIT)


def _mm_kernel(a_ref, w_ref, o_ref, a_bf):
    @pl.when(pl.program_id(1) == 0)
    def _():
        a_bf[...] = a_ref[...].astype(BF16)

    o_ref[...] = _dot(a_bf[...], w_ref[...]).astype(o_ref.dtype)


def matmul(a, w, tm, tn):
    M, K = a.shape
    N = w.shape[1]
    assert M % tm == 0 and N % tn == 0
    return pl.pallas_call(
        _mm_kernel,
        grid=(M // tm, N // tn),
        in_specs=[pl.BlockSpec((tm, K), lambda i, j: (i, 0)),
                  pl.BlockSpec((K, tn), lambda i, j: (0, j))],
        out_specs=pl.BlockSpec((tm, tn), lambda i, j: (i, j)),
        out_shape=jax.ShapeDtypeStruct((M, N), F32),
        scratch_shapes=[pltpu.VMEM((tm, K), BF16)],
        compiler_params=_params("parallel", "arbitrary"),
        name="matmul",
    )(a, w)


def _mm_ln_kernel(n_in, alpha, *refs):
    a_refs = refs[:n_in]
    w_ref, x_ref, g_ref, b_ref, o_ref = refs[n_in:]
    acc = None
    off = 0
    for a_ref in a_refs:
        k = a_ref.shape[1]
        part = _dot(a_ref[...].astype(BF16), w_ref[off:off + k, :])
        acc = part if acc is None else acc + part
        off += k
    o_ref[...] = _layer_norm(alpha * x_ref[...] + acc, g_ref[...], b_ref[...])


def matmul_residual_ln(a_list, w, x, g, b, alpha, tm):
    M, D = x.shape
    n_in = len(a_list)
    in_specs = [pl.BlockSpec((tm, a.shape[1]), lambda i: (i, 0)) for a in a_list]
    in_specs += [pl.BlockSpec(w.shape, lambda i: (0, 0)),
                 pl.BlockSpec((tm, D), lambda i: (i, 0)),
                 pl.BlockSpec((1, D), lambda i: (0, 0)),
                 pl.BlockSpec((1, D), lambda i: (0, 0))]
    return pl.pallas_call(
        functools.partial(_mm_ln_kernel, n_in, alpha),
        grid=(M // tm,),
        in_specs=in_specs,
        out_specs=pl.BlockSpec((tm, D), lambda i: (i, 0)),
        out_shape=jax.ShapeDtypeStruct((M, D), F32),
        compiler_params=_params("parallel"),
        name="matmul_residual_ln",
    )(*a_list, w, x, g.reshape(1, D), b.reshape(1, D))


def _compress_kernel(x_ref, pos_ref, w1_ref, w2_ref, o_ref):
    x = x_ref[...]
    half = x.shape[1]
    pos = pos_ref[...]
    lo = (x + pos[:, :half]).astype(BF16)
    hi = (x + pos[:, half:]).astype(BF16)
    p_lo = _dot(lo, w1_ref[:half, :])
    p_hi = _dot(hi, w1_ref[half:, :])
    h = p_lo + pltpu.roll(p_hi, x.shape[0] - 1, axis=0)
    o_ref[...] = _dot(_gelu(h).astype(BF16), w2_ref[...])


def nsa_compress(x16, pos, w1, w2):
    _, BG, n_blk, wd = x16.shape
    return pl.pallas_call(
        _compress_kernel,
        grid=(2, BG),
        in_specs=[pl.BlockSpec((None, None, n_blk, wd), lambda i, j: (i, j, 0, 0)),
                  pl.BlockSpec((None, 1, 2 * wd), lambda i, j: (i, 0, 0)),
                  pl.BlockSpec((None, 2 * wd, HEAD_DIM), lambda i, j: (i, 0, 0)),
                  pl.BlockSpec((None, HEAD_DIM, HEAD_DIM), lambda i, j: (i, 0, 0))],
        out_specs=pl.BlockSpec((None, None, n_blk, HEAD_DIM), lambda i, j: (i, j, 0, 0)),
        out_shape=jax.ShapeDtypeStruct((2, BG, n_blk, HEAD_DIM), F32),
        compiler_params=_params("parallel", "parallel"),
        name="nsa_compress",
    )(x16, pos, w1, w2)


def _cmp_attn_kernel(tq, q_ref, kc_ref, vc_ref, ov_ref, o_ref, sel_ref):
    qi = pl.program_id(2)
    scale = HEAD_DIM ** -0.5
    t = qi * tq + lax.broadcasted_iota(I32, (tq, LANES), 0)
    lane = lax.broadcasted_iota(I32, (tq, LANES), 1)
    vis = (lane * CMP_STRIDE + (CMP_BLOCK - 1)) <= t
    kc = kc_ref[...].astype(BF16)
    vc = vc_ref[...].astype(BF16)
    psum = jnp.zeros((tq, LANES), F32)
    for r in range(NSA_GROUP):
        q = q_ref[:, r * HEAD_DIM:(r + 1) * HEAD_DIM].astype(BF16)
        s = _dot_nt(q, kc) * scale
        s = jnp.where(vis, s, NEG_INF)
        m = jnp.max(s, axis=-1, keepdims=True)
        e = jnp.where(vis, jnp.exp(s - m), 0.0)
        l = jnp.sum(e, axis=-1, keepdims=True)
        p = e / jnp.where(l > 0.0, l, 1.0)
        o_ref[:, r * HEAD_DIM:(r + 1) * HEAD_DIM] = _dot(p.astype(BF16), vc)
        psum = psum + p
    p_hi = psum.astype(BF16)
    p_lo = (psum - p_hi.astype(F32)).astype(BF16)
    imp = _dot(p_hi, ov_ref[...]) + _dot(p_lo, ov_ref[...])
    n_slc = ov_ref.shape[0] * CMP_STRIDE // SLC_BLOCK
    cur = t // SLC_BLOCK
    forced = (lane == 0) | (lane == cur) | (lane == cur - 1)
    score = jnp.where(forced, FORCE_SCORE, imp)
    score = jnp.where(lane * SLC_BLOCK <= t, score, -1.0)
    score = jnp.where(lane < n_slc, score, -2.0)
    cnt = jnp.zeros((tq, LANES), I32)
    for i in range(n_slc):
        ci = score[:, i:i + 1]
        before = (ci > score) | ((ci == score) & (lane > i))
        cnt = cnt + before.astype(I32)
    sel = (cnt < min(SLC_TOPK, n_slc)) & (lane < n_slc)
    sel_ref[...] = sel.astype(F32)


def nsa_cmp_attention(proj, kv_cmp, overlap, B, S, tq):
    M = proj.shape[0]
    nq = S // tq
    G = NSA_KV_HEADS
    gw = NSA_GROUP * HEAD_DIM
    n_blk = kv_cmp.shape[2]
    return pl.pallas_call(
        functools.partial(_cmp_attn_kernel, tq),
        grid=(B, G, nq),
        in_specs=[pl.BlockSpec((tq, gw), lambda b, g, i: (b * nq + i, g)),
                  pl.BlockSpec((None, None, n_blk, HEAD_DIM), lambda b, g, i: (0, b * G + g, 0, 0)),
                  pl.BlockSpec((None, None, n_blk, HEAD_DIM), lambda b, g, i: (1, b * G + g, 0, 0)),
                  pl.BlockSpec(overlap.shape, lambda b, g, i: (0, 0))],
        out_specs=[pl.BlockSpec((tq, gw), lambda b, g, i: (b * nq + i, g)),
                   pl.BlockSpec((None, tq, LANES), lambda b, g, i: (b * G + g, i, 0))],
        out_shape=[jax.ShapeDtypeStruct((M, NSA_HEADS * HEAD_DIM), F32),
                   jax.ShapeDtypeStruct((B * G, S, LANES), F32)],
        compiler_params=_params("parallel", "parallel", "parallel"),
        name="nsa_cmp_attention",
    )(proj, kv_cmp, kv_cmp, overlap)


SEL_PREFIX_CLASSES = 4


def _sel_win_kernel(tq, q_ref, gl_ref, bg_ref, oc_ref, sel_ref, ks_ref, vs_ref, kw_ref, vw_ref,
                    cos_ref, sin_ref, ex_ref, o_ref, ks_r, kw_r, vs_b, vw_b):
    qi = pl.program_id(2)
    nq = pl.num_programs(2)
    S = ks_ref.shape[0]
    scale = HEAD_DIM ** -0.5
    span = WINDOW + tq
    rows = NSA_GROUP * tq

    @pl.when(qi == 0)
    def _():
        cos = cos_ref[...]
        sin = sin_ref[...]
        ks_r[...] = _rope(ks_ref[...], cos, sin).astype(BF16)
        kw_r[...] = _rope(kw_ref[...], cos, sin).astype(BF16)
        vs_b[...] = vs_ref[...].astype(BF16)
        vw_b[...] = vw_ref[...].astype(BF16)

    q0 = pl.multiple_of(qi * tq, tq)
    cos_q = cos_ref[pl.ds(q0, tq), :]
    sin_q = sin_ref[pl.ds(q0, tq), :]
    q = jnp.concatenate(
        [_rope(q_ref[:, r * HEAD_DIM:(r + 1) * HEAD_DIM], cos_q, sin_q).astype(BF16) for r in range(NSA_GROUP)],
        axis=0)
    sel = jnp.concatenate([sel_ref[...].astype(BF16)] * NSA_GROUP, axis=0)
    gate = jax.nn.sigmoid(gl_ref[...] + bg_ref[...])

    w0 = pl.multiple_of(jnp.maximum(q0 - WINDOW, 0), tq)
    t_w = q0 + (lax.broadcasted_iota(I32, (rows, span), 0) & (tq - 1))
    k_w = w0 + lax.broadcasted_iota(I32, (rows, span), 1)
    mask_w = (k_w <= t_w) & (t_w - k_w < WINDOW)
    s = jnp.where(mask_w, _dot_nt(q, kw_r[pl.ds(w0, span), :]) * scale, NEG_INF)
    o_w = _dot(_softmax_rows(s).astype(BF16), vw_b[pl.ds(w0, span), :])

    def selected_branch(n_keys):
        sel_keys = _dot(sel, ex_ref[:, :n_keys]) > 0.5
        t_s = q0 + (lax.broadcasted_iota(I32, (rows, n_keys), 0) & (tq - 1))
        k_s = lax.broadcasted_iota(I32, (rows, n_keys), 1)
        s = jnp.where(sel_keys & (k_s <= t_s), _dot_nt(q, ks_r[:n_keys, :]) * scale, NEG_INF)
        o_s = _dot(_softmax_rows(s).astype(BF16), vs_b[:n_keys, :])
        for r in range(NSA_GROUP):
            cols = slice(r * HEAD_DIM, (r + 1) * HEAD_DIM)
            part = slice(r * tq, (r + 1) * tq)
            o_ref[:, cols] = (gate[:, 3 * r:3 * r + 1] * oc_ref[:, cols]
                              + gate[:, 3 * r + 1:3 * r + 2] * o_s[part]
                              + gate[:, 3 * r + 2:3 * r + 3] * o_w[part])

    per_class = nq // SEL_PREFIX_CLASSES
    for c in range(SEL_PREFIX_CLASSES):
        @pl.when(qi // per_class == c)
        def _(c=c):
            selected_branch((c + 1) * per_class * tq)


def nsa_sel_win_attention(proj, b_gate2, o_cmp, sel, cos, sin, expand, B, S, tq):
    M = proj.shape[0]
    nq = S // tq
    G = NSA_KV_HEADS
    gw = NSA_GROUP * HEAD_DIM

    def kv_spec(cb):
        return pl.BlockSpec((S, HEAD_DIM), lambda b, g, i: (b, cb + g))

    return pl.pallas_call(
        functools.partial(_sel_win_kernel, tq),
        grid=(B, G, nq),
        in_specs=[pl.BlockSpec((tq, gw), lambda b, g, i: (b * nq + i, g)),
                  pl.BlockSpec((tq, LANES), lambda b, g, i: (b * nq + i, CB_GATE + g)),
                  pl.BlockSpec((None, 1, LANES), lambda b, g, i: (g, 0, 0)),
                  pl.BlockSpec((tq, gw), lambda b, g, i: (b * nq + i, g)),
                  pl.BlockSpec((None, tq, LANES), lambda b, g, i: (b * G + g, i, 0)),
                  kv_spec(CB_KS), kv_spec(CB_VS), kv_spec(CB_KW), kv_spec(CB_VW),
                  pl.BlockSpec((S, HEAD_DIM), lambda b, g, i: (0, 0)),
                  pl.BlockSpec((S, HEAD_DIM), lambda b, g, i: (0, 0)),
                  pl.BlockSpec(expand.shape, lambda b, g, i: (0, 0))],
        out_specs=pl.BlockSpec((tq, gw), lambda b, g, i: (b * nq + i, g)),
        out_shape=jax.ShapeDtypeStruct((M, NSA_HEADS * HEAD_DIM), F32),
        scratch_shapes=[pltpu.VMEM((S, HEAD_DIM), BF16)] * 4,
        compiler_params=_params("parallel", "parallel", "arbitrary"),
        name="nsa_sel_win_attention",
    )(proj, proj, b_gate2, o_cmp, sel, proj, proj, proj, proj, cos, sin, expand)


def _retention_kernel(q_ref, k_ref, v_ref, gate_ref, cos_ref, sin_ref, dm_ref, xi_ref, zeta_ref,
                      cdec_ref, gng_ref, gnb_ref, o_ref, state):
    @pl.when(pl.program_id(1) == 0)
    def _():
        state[...] = jnp.zeros_like(state)

    cos = cos_ref[...]
    sin = sin_ref[...]
    for h in range(RET_HEADS):
        cols = slice(h * HEAD_DIM, (h + 1) * HEAD_DIM)
        q = _rope(q_ref[:, cols], cos, sin)
        k = _rope(k_ref[:, cols], cos, sin) * (HEAD_DIM ** -0.5)
        qb = q.astype(BF16)
        vb = v_ref[:, cols].astype(BF16)
        inner = _dot_nt(qb, k.astype(BF16)) * dm_ref[h]
        r_old = state[h]
        o = _dot(inner.astype(BF16), vb) + _dot(qb, r_old.astype(BF16)) * xi_ref[h]
        state[h] = r_old * cdec_ref[h] + _dot_tn((k * zeta_ref[h]).astype(BF16), vb)
        mu = jnp.mean(o, axis=-1, keepdims=True)
        d = o - mu
        var = jnp.mean(d * d, axis=-1, keepdims=True)
        y = d * lax.rsqrt(var + GN_EPS) * gng_ref[:, cols] + gnb_ref[:, cols]
        gate = gate_ref[:, cols]
        o_ref[:, cols] = gate * jax.nn.sigmoid(gate) * y


def retention(proj, cos, sin, dmask, xi, zeta, cdec, gn_g, gn_b, B, S):
    M = proj.shape[0]
    C = RET_CHUNK
    nc = S // C
    H = RET_HEADS
    W = H * HEAD_DIM

    def col_spec(cb):
        return pl.BlockSpec((C, W), lambda b, c: (b * nc + c, cb // H))

    def table_spec(rows):
        return pl.BlockSpec((H, rows, HEAD_DIM), lambda b, c: (0, 0, 0))

    return pl.pallas_call(
        _retention_kernel,
        grid=(B, nc),
        in_specs=[col_spec(CB_RQ), col_spec(CB_RK), col_spec(CB_RV), col_spec(CB_RG),
                  pl.BlockSpec((C, HEAD_DIM), lambda b, c: (c, 0)),
                  pl.BlockSpec((C, HEAD_DIM), lambda b, c: (c, 0)),
                  table_spec(C), table_spec(C), table_spec(C), table_spec(1),
                  pl.BlockSpec((1, W), lambda b, c: (0, 0)),
                  pl.BlockSpec((1, W), lambda b, c: (0, 0))],
        out_specs=pl.BlockSpec((C, W), lambda b, c: (b * nc + c, 0)),
        out_shape=jax.ShapeDtypeStruct((M, W), F32),
        scratch_shapes=[pltpu.VMEM((H, HEAD_DIM, HEAD_DIM), F32)],
        compiler_params=_params("parallel", "arbitrary"),
        name="retention",
    )(proj, proj, proj, proj, cos, sin, dmask, xi, zeta, cdec, gn_g, gn_b)


def _xattn_kernel(q_ref, k_ref, v_ref, o_ref):
    hd = q_ref.shape[1] // XA_HEADS
    scale = hd ** -0.5
    for h in range(XA_HEADS):
        cols = slice(h * hd, (h + 1) * hd)
        s = _dot_nt(q_ref[:, cols].astype(BF16), k_ref[:, cols].astype(BF16)) * scale
        o_ref[:, cols] = _dot(_softmax_rows(s).astype(BF16), v_ref[:, cols].astype(BF16))


def cross_attention(q, kv, B, S, n_mem, tq):
    M, D = q.shape
    nq = S // tq
    return pl.pallas_call(
        _xattn_kernel,
        grid=(B, nq),
        in_specs=[pl.BlockSpec((tq, D), lambda b, i: (b * nq + i, 0)),
                  pl.BlockSpec((n_mem, D), lambda b, i: (b, 0)),
                  pl.BlockSpec((n_mem, D), lambda b, i: (b, 1))],
        out_specs=pl.BlockSpec((tq, D), lambda b, i: (b * nq + i, 0)),
        out_shape=jax.ShapeDtypeStruct((M, D), F32),
        compiler_params=_params("parallel", "parallel"),
        name="cross_attention",
    )(q, kv, kv)


def _top_k_rows(x, k, code):
    big = jnp.iinfo(jnp.int32).max
    vals, codes = [], []
    cur = x
    for _ in range(k):
        m = jnp.max(cur, axis=0, keepdims=True)
        c = jnp.min(jnp.where(cur == m, code, big), axis=0, keepdims=True)
        vals.append(m)
        codes.append(c)
        cur = jnp.where(code == c, -jnp.inf, cur)
    return jnp.concatenate(vals, axis=0), jnp.concatenate(codes, axis=0)


def _take_rows(table, idx, n):
    out = jnp.zeros(idx.shape, table.dtype)
    for p in range(n):
        out = jnp.where(idx == p, table[p:p + 1, :], out)
    return out


def _pair_candidates(s1, s2):
    K = PEER_TOPK
    tm = s1.shape[1]
    half = K // 2
    p_full = lax.broadcasted_iota(I32, (K, tm), 0)
    p_half = lax.broadcasted_iota(I32, (half, tm), 0)
    vals = [s1 + s2[0:1, :]]
    codes = [p_full * K]
    for q in range(1, half):
        n_valid = K // (q + 1)
        vals.append(jnp.where(p_half < n_valid, s1[:half, :] + s2[q:q + 1, :], -jnp.inf))
        codes.append(p_half * K + q)
    vals.append(s1[0:1, :] + s2[half:, :])
    codes.append(p_half + half)
    return jnp.concatenate(vals, axis=0), jnp.concatenate(codes, axis=0)


def _peer_topk_kernel(q_ref, keys_ref, a_ref, b_ref, g_ref):
    K = PEER_TOPK
    tm = q_ref.shape[0]
    key_row = lax.broadcasted_iota(I32, (PEER_NKEYS, tm), 0)
    for h in range(PEER_HEADS):
        halves = []
        for p in range(2):
            c0 = (2 * h + p) * PEER_HALF
            qhp = q_ref[:, c0:c0 + PEER_HALF].astype(BF16)
            s = _dot_nt(keys_ref[2 * h + p], qhp)
            halves.append(_top_k_rows(s, K, key_row))
        (s1, i1), (s2, i2) = halves
        cand, code = _pair_candidates(s1, s2)
        top, pos = _top_k_rows(cand, K, code)
        a_ref[h * K:(h + 1) * K, :] = _take_rows(i1, pos // K, K)
        b_ref[h * K:(h + 1) * K, :] = _take_rows(i2, pos % K, K)
        e = jnp.exp(top - jnp.max(top, axis=0, keepdims=True))
        g_ref[h * K:(h + 1) * K, :] = e / jnp.sum(e, axis=0, keepdims=True)


def peer_topk(q, keys, tm):
    T, D = q.shape
    HK = PEER_HEADS * PEER_TOPK
    out = jax.ShapeDtypeStruct((HK, T), I32)
    spec = pl.BlockSpec((HK, tm), lambda i: (0, i))
    return pl.pallas_call(
        _peer_topk_kernel,
        grid=(T // tm,),
        in_specs=[pl.BlockSpec((tm, D), lambda i: (i, 0)),
                  pl.BlockSpec(keys.shape, lambda i: (0, 0, 0))],
        out_specs=[spec, spec, spec],
        out_shape=[out, out, jax.ShapeDtypeStruct((HK, T), F32)],
        compiler_params=_params("parallel"),
        name="peer_topk",
    )(q, keys)


GATE_MAP_UNROLL = 8


def _peer_gate_map_kernel(tg, a_ref, b_ref, g_ref, o_ref):
    row = lax.broadcasted_iota(I32, (PEER_NKEYS, LANES), 0)

    def body(i, carry):
        t0 = pl.multiple_of(i * GATE_MAP_UNROLL, GATE_MAP_UNROLL)
        a8 = a_ref[pl.ds(t0, GATE_MAP_UNROLL), :]
        b8 = b_ref[pl.ds(t0, GATE_MAP_UNROLL), :]
        g8 = g_ref[pl.ds(t0, GATE_MAP_UNROLL), :]
        g8_hi = g8.astype(BF16).astype(F32)
        g8_lo = g8 - g8_hi
        for u in range(GATE_MAP_UNROLL):
            onehot_a = jnp.where(a8[u:u + 1, :] == row, 1.0, 0.0).astype(BF16)
            hit_b = b8[u:u + 1, :] == row
            b_hi = jnp.where(hit_b, g8_hi[u:u + 1, :], 0.0).astype(BF16)
            b_lo = jnp.where(hit_b, g8_lo[u:u + 1, :], 0.0).astype(BF16)
            gm = _dot_nt(jnp.concatenate([onehot_a, onehot_a], axis=1), jnp.concatenate([b_hi, b_lo], axis=1))
            o_ref[pl.ds(t0 + u, PEER_NKEYS, stride=tg), :] = gm
        return carry

    lax.fori_loop(0, tg // GATE_MAP_UNROLL, body, 0)


def peer_gate_map(a, b, g, tg):
    T, HK = a.shape
    spec = pl.BlockSpec((tg, HK), lambda i: (i, 0))
    out = pl.pallas_call(
        functools.partial(_peer_gate_map_kernel, tg),
        grid=(T // tg,),
        in_specs=[spec, spec, spec],
        out_specs=pl.BlockSpec((PEER_NKEYS * tg, PEER_NKEYS), lambda i: (i, 0)),
        out_shape=jax.ShapeDtypeStruct((T * PEER_NKEYS, PEER_NKEYS), F32),
        compiler_params=_params("parallel"),
        name="peer_gate_map",
    )(a, b, g)
    return out.reshape(T // tg, PEER_NKEYS, tg, PEER_NKEYS)


def _peer_ffn_kernel(alpha, x_ref, u_ref, v_ref, gm_ref, lg_ref, lb_ref, o_ref, x_bf, h_scr, w_bf, acc):
    j = pl.program_id(1)
    n_tiles = pl.num_programs(1) - 1
    n_sub, nr, tg, _ = gm_ref.shape
    slot = j % 2

    def project(dst):
        h_scr[dst] = _dot_nt(x_bf[...], u_ref[...])

    def gate_and_accumulate(src):
        for r in range(nr):
            cols = slice(r * PEER_NKEYS, (r + 1) * PEER_NKEYS)
            for s in range(n_sub):
                rows = slice(s * tg, (s + 1) * tg)
                w_bf[rows, cols] = (gm_ref[s, r] * _gelu(h_scr[src, rows, cols])).astype(BF16)
        acc[...] += _dot(w_bf[...], v_ref[...])

    @pl.when(j == 0)
    def _():
        x_bf[...] = x_ref[...].astype(BF16)
        acc[...] = jnp.zeros_like(acc)
        project(0)

    @pl.when((j > 0) & (j < n_tiles))
    def _():
        project(slot)
        gate_and_accumulate(1 - slot)

    @pl.when(j == n_tiles)
    def _():
        gate_and_accumulate(1 - slot)
        o_ref[...] = _layer_norm(alpha * x_ref[...] + acc[...], lg_ref[...], lb_ref[...])


def peer_ffn(x, u, v, gate_map, ln_g, ln_b, alpha, tm, nr):
    T, D = x.shape
    NE = u.shape[0]
    te = nr * PEER_NKEYS
    tg = gate_map.shape[2]
    n_tiles = NE // te
    return pl.pallas_call(
        functools.partial(_peer_ffn_kernel, alpha),
        grid=(T // tm, n_tiles + 1),
        in_specs=[pl.BlockSpec((tm, D), lambda i, j: (i, 0)),
                  pl.BlockSpec((te, D), lambda i, j: (jnp.minimum(j, n_tiles - 1), 0)),
                  pl.BlockSpec((te, D), lambda i, j: (jnp.maximum(j - 1, 0), 0)),
                  pl.BlockSpec((tm // tg, nr, tg, PEER_NKEYS), lambda i, j: (i, jnp.maximum(j - 1, 0), 0, 0)),
                  pl.BlockSpec((1, D), lambda i, j: (0, 0)),
                  pl.BlockSpec((1, D), lambda i, j: (0, 0))],
        out_specs=pl.BlockSpec((tm, D), lambda i, j: (i, 0)),
        out_shape=jax.ShapeDtypeStruct((T, D), F32),
        scratch_shapes=[pltpu.VMEM((tm, D), BF16), pltpu.VMEM((2, tm, te), F32), pltpu.VMEM((tm, te), BF16),
                        pltpu.VMEM((tm, D), F32)],
        compiler_params=_params("parallel", "arbitrary"),
        name="peer_ffn",
    )(x, u, v, gate_map, ln_g.reshape(1, D), ln_b.reshape(1, D))


def _rope_tables(S):
    inv = 1.0 / (ROPE_THETA ** (jnp.arange(0, HEAD_DIM, 2, dtype=F32) / HEAD_DIM))
    ang = jnp.arange(S, dtype=F32)[:, None] * inv[None, :]
    cos = jnp.cos(ang)
    sin = jnp.sin(ang)
    return jnp.concatenate([cos, cos], -1), jnp.concatenate([-sin, sin], -1)


def _retention_tables():
    H, C = RET_HEADS, RET_CHUNK
    log_g = jnp.log(1.0 - 2.0 ** (-5.0 - jnp.arange(H, dtype=F32)))
    i = jnp.arange(C, dtype=F32)
    diff = i[:, None] - i[None, :]
    causal = diff >= 0
    dmask = jnp.where(causal[None], jnp.exp(jnp.where(causal, diff, 0.0)[None] * log_g[:, None, None]), 0.0)
    xi = jnp.exp((i[None, :] + 1.0) * log_g[:, None])
    zeta = jnp.exp((C - 1.0 - i[None, :]) * log_g[:, None])
    cdec = jnp.exp(C * log_g)
    wide = (H, C, HEAD_DIM)
    return (dmask, jnp.broadcast_to(xi[:, :, None], wide), jnp.broadcast_to(zeta[:, :, None], wide),
            jnp.broadcast_to(cdec[:, None, None], (H, 1, HEAD_DIM)))


def _overlap_matrix(n_blk, S):
    n = np.arange(n_blk)[:, None] * CMP_STRIDE
    j = np.arange(LANES)[None, :] * SLC_BLOCK
    n_cmp = (S - CMP_BLOCK) // CMP_STRIDE + 1
    ov = (n < j + SLC_BLOCK) & (n + CMP_BLOCK > j) & (np.arange(n_blk)[:, None] < n_cmp) & (j < S)
    return jnp.asarray(ov, BF16)


def _expand_matrix(S):
    ex = (np.arange(S)[None, :] // SLC_BLOCK) == np.arange(LANES)[:, None]
    return jnp.asarray(ex, BF16)


def _pack_w_in(w_in, b_gate):
    D = w_in.shape[0]
    nsa_w = NSA_HEADS * HEAD_DIM
    kv_w = 6 * NSA_KV_HEADS * HEAD_DIM
    n_gate = 3 * NSA_HEADS
    per_group = n_gate // NSA_KV_HEADS
    gate_w = w_in[:, nsa_w + kv_w:nsa_w + kv_w + n_gate].reshape(D, NSA_KV_HEADS, per_group)
    gate_w = jnp.pad(gate_w, ((0, 0), (0, 0), (0, LANES - per_group))).reshape(D, NSA_KV_HEADS * LANES)
    pad_w = jnp.zeros((D, (CB_RQ - CB_PAD) * LANES), w_in.dtype)
    packed = jnp.concatenate([w_in[:, :nsa_w + kv_w], gate_w, pad_w, w_in[:, nsa_w + kv_w + n_gate:]], axis=1)
    bg = jnp.pad(b_gate.reshape(NSA_KV_HEADS, 1, per_group), ((0, 0), (0, 0), (0, LANES - per_group)))
    return packed.astype(BF16), bg


def _mixer(x2, B, S, w_in, b_gate, cmp_pos, cmp_w1, cmp_w2, gn_g, gn_b, w_out, ln_g, ln_b, alpha, tables):
    cos, sin, ret_tabs, expand = tables
    G = NSA_KV_HEADS
    w_packed, bg = _pack_w_in(w_in, b_gate)
    proj = matmul(x2, w_packed, tm=1024, tn=1024)
    n_blk = S // CMP_STRIDE
    kv = proj[:, CB_KC * LANES:CB_KS * LANES].reshape(B, n_blk, CMP_STRIDE, 2, G, HEAD_DIM)
    x16 = kv.transpose(3, 0, 4, 1, 2, 5).reshape(2, B * G, n_blk, CMP_STRIDE * HEAD_DIM)
    kv_cmp = nsa_compress(x16, cmp_pos.reshape(2, 1, CMP_BLOCK * HEAD_DIM), cmp_w1.astype(BF16),
                          cmp_w2.astype(BF16))
    o_cmp, sel = nsa_cmp_attention(proj, kv_cmp, _overlap_matrix(n_blk, S), B, S, tq=256)
    o_nsa = nsa_sel_win_attention(proj, bg, o_cmp, sel, cos, sin, expand, B, S, tq=128)
    o_ret = retention(proj, cos, sin, *ret_tabs, gn_g.reshape(1, -1), gn_b.reshape(1, -1), B, S)
    return matmul_residual_ln([o_nsa, o_ret], w_out.astype(BF16), x2, ln_g, ln_b, alpha, tm=512)


def _cross(x2, mem2, B, S, wq, wk, wv, wo, ln_g, ln_b, alpha):
    n_mem = mem2.shape[0] // B
    q = matmul(x2, wq.astype(BF16), tm=1024, tn=1024)
    kv = matmul(mem2, jnp.concatenate([wk, wv], axis=1).astype(BF16), tm=mem2.shape[0], tn=1024)
    o = cross_attention(q, kv, B, S, n_mem, tq=256)
    return matmul_residual_ln([o], wo.astype(BF16), x2, ln_g, ln_b, alpha, tm=512)


def _peer(x2, w_q, sub_keys, u_tab, v_tab, ln_g, ln_b, alpha):
    q = matmul(x2, w_q.astype(BF16), tm=1024, tn=1024)
    keys = sub_keys.reshape(PEER_HEADS * 2, PEER_NKEYS, PEER_HALF).astype(BF16)
    a, b, g = peer_topk(q, keys, tm=256)
    gate_map = peer_gate_map(a.T, b.T, g.T, tg=128)
    return peer_ffn(x2, u_tab.astype(BF16), v_tab.astype(BF16), gate_map, ln_g, ln_b, alpha, tm=512, nr=8)


def kernel(x, mem, w_in, b_gate, cmp_pos, cmp_w1, cmp_w2, ret_gn_g, ret_gn_b, w_mix_out, ln1_g, ln1_b,
           xa_wq, xa_wk, xa_wv, xa_wo, ln2_g, ln2_b, peer_wq, peer_sub_keys, peer_u, peer_v, ln3_g, ln3_b):
    B, S, D = x.shape
    depth = w_in.shape[0]
    alpha = (2 * depth) ** 0.25
    tables = (*_rope_tables(S), _retention_tables(), _expand_matrix(S))
    x2 = x.reshape(B * S, D)
    mem2 = mem.reshape(-1, D)
    for l in range(depth):
        x2 = _mixer(x2, B, S, w_in[l], b_gate[l], cmp_pos[l], cmp_w1[l], cmp_w2[l], ret_gn_g[l], ret_gn_b[l],
                    w_mix_out[l], ln1_g[l], ln1_b[l], alpha, tables)
        x2 = _cross(x2, mem2, B, S, xa_wq[l], xa_wk[l], xa_wv[l], xa_wo[l], ln2_g[l], ln2_b[l], alpha)
        x2 = _peer(x2, peer_wq[l], peer_sub_keys[l], peer_u[l], peer_v[l], ln3_g[l], ln3_b[l], alpha)
    return x2.reshape(B, S, D)
```

```python
import functools

import numpy as np
import jax
import jax.numpy as jnp
from jax import lax
from jax.experimental import pallas as pl
from jax.experimental.pallas import tpu as pltpu

F32 = jnp.float32
BF16 = jnp.bfloat16
I32 = jnp.int32

HEAD_DIM = 128
LANES = 128
NSA_HEADS = 8
NSA_KV_HEADS = 2
NSA_GROUP = NSA_HEADS // NSA_KV_HEADS
RET_HEADS = 8
CMP_BLOCK = 32
CMP_STRIDE = 16
SLC_BLOCK = 64
SLC_TOPK = 16
WINDOW = 512
RET_CHUNK = 128
XA_HEADS = 4
PEER_HEADS = 8
PEER_NKEYS = 128
PEER_HALF = 128
PEER_TOPK = 16
ROPE_THETA = 10000.0
LN_EPS = 1e-5
GN_EPS = 1e-5
NEG_INF = -1e30
FORCE_SCORE = 1e9
VMEM_LIMIT = 56 * 1024 * 1024

CB_Q = 0
CB_KC, CB_VC, CB_KS, CB_VS, CB_KW, CB_VW = 8, 10, 12, 14, 16, 18
CB_GATE = 20
CB_PAD = 22
CB_RQ, CB_RK, CB_RV, CB_RG = 24, 32, 40, 48
P_PACKED = 56 * LANES


def _dot(a, b):
    return jnp.dot(a, b, preferred_element_type=F32)


def _dot_nt(a, b):
    return lax.dot_general(a, b, (((1,), (1,)), ((), ())), preferred_element_type=F32)


def _dot_tn(a, b):
    return lax.dot_general(a, b, (((0,), (0,)), ((), ())), preferred_element_type=F32)


def _gelu(x):
    return 0.5 * x * (1.0 + lax.erf(x * (2.0 ** -0.5)))


def _rope(x, cos, sin_signed):
    return x * cos + pltpu.roll(x, HEAD_DIM // 2, axis=1) * sin_signed


def _layer_norm(y, g, b):
    mu = jnp.mean(y, axis=-1, keepdims=True)
    d = y - mu
    var = jnp.mean(d * d, axis=-1, keepdims=True)
    return d * lax.rsqrt(var + LN_EPS) * g + b


def _softmax_rows(s):
    m = jnp.max(s, axis=-1, keepdims=True)
    e = jnp.exp(s - m)
    return e / jnp.sum(e, axis=-1, keepdims=True)


def _params(*sem):
    return pltpu.CompilerParams(dimension_semantics=sem, vmem_limit_bytes=VMEM_LIMIT)


CAST_ROWS = 1024


def _cast_kernel(x_ref, o_ref):
    o_ref[...] = x_ref[...].astype(o_ref.dtype)


def cast_layer_bf16(stacked, l):
    _, R, C = stacked.shape
    tr = min(CAST_ROWS, R)
    return pl.pallas_call(
        _cast_kernel,
        grid=(R // tr,),
        in_specs=[pl.BlockSpec((None, tr, C), lambda i: (l, i, 0))],
        out_specs=pl.BlockSpec((tr, C), lambda i: (i, 0)),
        out_shape=jax.ShapeDtypeStruct((R, C), BF16),
        compiler_params=_params("parallel"),
        name="cast_bf16",
    )(stacked)


def _mm_kernel(a_ref, w_ref, o_ref, a_bf):
    @pl.when(pl.program_id(1) == 0)
    def _():
        a_bf[...] = a_ref[...].astype(BF16)

    o_ref[...] = _dot(a_bf[...], w_ref[...]).astype(o_ref.dtype)


def matmul(a, w, tm, tn):
    M, K = a.shape
    N = w.shape[1]
    assert M % tm == 0 and N % tn == 0
    return pl.pallas_call(
        _mm_kernel,
        grid=(M // tm, N // tn),
        in_specs=[pl.BlockSpec((tm, K), lambda i, j: (i, 0)),
                  pl.BlockSpec((K, tn), lambda i, j: (0, j))],
        out_specs=pl.BlockSpec((tm, tn), lambda i, j: (i, j)),
        out_shape=jax.ShapeDtypeStruct((M, N), F32),
        scratch_shapes=[pltpu.VMEM((tm, K), BF16)],
        compiler_params=_params("parallel", "arbitrary"),
        name="matmul",
    )(a, w)


def _mm_ln_kernel(n_in, alpha, *refs):
    a_refs = refs[:n_in]
    w_ref, x_ref, g_ref, b_ref, o_ref = refs[n_in:]
    acc = None
    off = 0
    for a_ref in a_refs:
        k = a_ref.shape[1]
        part = _dot(a_ref[...].astype(BF16), w_ref[off:off + k, :])
        acc = part if acc is None else acc + part
        off += k
    o_ref[...] = _layer_norm(alpha * x_ref[...] + acc, g_ref[...], b_ref[...])


def matmul_residual_ln(a_list, w, x, g, b, alpha, tm):
    M, D = x.shape
    n_in = len(a_list)
    in_specs = [pl.BlockSpec((tm, a.shape[1]), lambda i: (i, 0)) for a in a_list]
    in_specs += [pl.BlockSpec(w.shape, lambda i: (0, 0)),
                 pl.BlockSpec((tm, D), lambda i: (i, 0)),
                 pl.BlockSpec((1, D), lambda i: (0, 0)),
                 pl.BlockSpec((1, D), lambda i: (0, 0))]
    return pl.pallas_call(
        functools.partial(_mm_ln_kernel, n_in, alpha),
        grid=(M // tm,),
        in_specs=in_specs,
        out_specs=pl.BlockSpec((tm, D), lambda i: (i, 0)),
        out_shape=jax.ShapeDtypeStruct((M, D), F32),
        compiler_params=_params("parallel"),
        name="matmul_residual_ln",
    )(*a_list, w, x, g.reshape(1, D), b.reshape(1, D))


def _compress_kernel(x_ref, pos_ref, w1_ref, w2_ref, o_ref):
    x = x_ref[...]
    half = x.shape[1]
    pos = pos_ref[...]
    lo = (x + pos[:, :half]).astype(BF16)
    hi = (x + pos[:, half:]).astype(BF16)
    p_lo = _dot(lo, w1_ref[:half, :])
    p_hi = _dot(hi, w1_ref[half:, :])
    h = p_lo + pltpu.roll(p_hi, x.shape[0] - 1, axis=0)
    o_ref[...] = _dot(_gelu(h).astype(BF16), w2_ref[...])


def nsa_compress(x16, pos, w1, w2):
    _, BG, n_blk, wd = x16.shape
    return pl.pallas_call(
        _compress_kernel,
        grid=(2, BG),
        in_specs=[pl.BlockSpec((None, None, n_blk, wd), lambda i, j: (i, j, 0, 0)),
                  pl.BlockSpec((None, 1, 2 * wd), lambda i, j: (i, 0, 0)),
                  pl.BlockSpec((None, 2 * wd, HEAD_DIM), lambda i, j: (i, 0, 0)),
                  pl.BlockSpec((None, HEAD_DIM, HEAD_DIM), lambda i, j: (i, 0, 0))],
        out_specs=pl.BlockSpec((None, None, n_blk, HEAD_DIM), lambda i, j: (i, j, 0, 0)),
        out_shape=jax.ShapeDtypeStruct((2, BG, n_blk, HEAD_DIM), F32),
        compiler_params=_params("parallel", "parallel"),
        name="nsa_compress",
    )(x16, pos, w1, w2)


def _cmp_attn_kernel(tq, q_ref, kc_ref, vc_ref, ov_ref, o_ref, sel_ref):
    qi = pl.program_id(2)
    scale = HEAD_DIM ** -0.5
    t = qi * tq + lax.broadcasted_iota(I32, (tq, LANES), 0)
    lane = lax.broadcasted_iota(I32, (tq, LANES), 1)
    vis = (lane * CMP_STRIDE + (CMP_BLOCK - 1)) <= t
    kc = kc_ref[...].astype(BF16)
    vc = vc_ref[...].astype(BF16)
    psum = jnp.zeros((tq, LANES), F32)
    for r in range(NSA_GROUP):
        q = q_ref[:, r * HEAD_DIM:(r + 1) * HEAD_DIM].astype(BF16)
        s = _dot_nt(q, kc) * scale
        s = jnp.where(vis, s, NEG_INF)
        m = jnp.max(s, axis=-1, keepdims=True)
        e = jnp.where(vis, jnp.exp(s - m), 0.0)
        l = jnp.sum(e, axis=-1, keepdims=True)
        p = e / jnp.where(l > 0.0, l, 1.0)
        o_ref[:, r * HEAD_DIM:(r + 1) * HEAD_DIM] = _dot(p.astype(BF16), vc)
        psum = psum + p
    p_hi = psum.astype(BF16)
    p_lo = (psum - p_hi.astype(F32)).astype(BF16)
    imp = _dot(p_hi, ov_ref[...]) + _dot(p_lo, ov_ref[...])
    n_slc = ov_ref.shape[0] * CMP_STRIDE // SLC_BLOCK
    cur = t // SLC_BLOCK
    forced = (lane == 0) | (lane == cur) | (lane == cur - 1)
    score = jnp.where(forced, FORCE_SCORE, imp)
    score = jnp.where(lane * SLC_BLOCK <= t, score, -1.0)
    score = jnp.where(lane < n_slc, score, -2.0)
    cnt = jnp.zeros((tq, LANES), I32)
    for i in range(n_slc):
        ci = score[:, i:i + 1]
        before = (ci > score) | ((ci == score) & (lane > i))
        cnt = cnt + before.astype(I32)
    sel = (cnt < min(SLC_TOPK, n_slc)) & (lane < n_slc)
    sel_ref[...] = sel.astype(F32)


def nsa_cmp_attention(proj, kv_cmp, overlap, B, S, tq):
    M = proj.shape[0]
    nq = S // tq
    G = NSA_KV_HEADS
    gw = NSA_GROUP * HEAD_DIM
    n_blk = kv_cmp.shape[2]
    return pl.pallas_call(
        functools.partial(_cmp_attn_kernel, tq),
        grid=(B, G, nq),
        in_specs=[pl.BlockSpec((tq, gw), lambda b, g, i: (b * nq + i, g)),
                  pl.BlockSpec((None, None, n_blk, HEAD_DIM), lambda b, g, i: (0, b * G + g, 0, 0)),
                  pl.BlockSpec((None, None, n_blk, HEAD_DIM), lambda b, g, i: (1, b * G + g, 0, 0)),
                  pl.BlockSpec(overlap.shape, lambda b, g, i: (0, 0))],
        out_specs=[pl.BlockSpec((tq, gw), lambda b, g, i: (b * nq + i, g)),
                   pl.BlockSpec((None, tq, LANES), lambda b, g, i: (b * G + g, i, 0))],
        out_shape=[jax.ShapeDtypeStruct((M, NSA_HEADS * HEAD_DIM), F32),
                   jax.ShapeDtypeStruct((B * G, S, LANES), F32)],
        compiler_params=_params("parallel", "parallel", "parallel"),
        name="nsa_cmp_attention",
    )(proj, kv_cmp, kv_cmp, overlap)


SEL_PREFIX_CLASSES = 4


def _sel_win_kernel(tq, q_ref, gl_ref, bg_ref, oc_ref, sel_ref, ks_ref, vs_ref, kw_ref, vw_ref,
                    cos_ref, sin_ref, ex_ref, o_ref, ks_r, kw_r, vs_b, vw_b):
    qi = pl.program_id(2)
    nq = pl.num_programs(2)
    S = ks_ref.shape[0]
    scale = HEAD_DIM ** -0.5
    span = WINDOW + tq
    rows = NSA_GROUP * tq

    @pl.when(qi == 0)
    def _():
        cos = cos_ref[...]
        sin = sin_ref[...]
        ks_r[...] = _rope(ks_ref[...], cos, sin).astype(BF16)
        kw_r[...] = _rope(kw_ref[...], cos, sin).astype(BF16)
        vs_b[...] = vs_ref[...].astype(BF16)
        vw_b[...] = vw_ref[...].astype(BF16)

    q0 = pl.multiple_of(qi * tq, tq)
    cos_q = cos_ref[pl.ds(q0, tq), :]
    sin_q = sin_ref[pl.ds(q0, tq), :]
    q = jnp.concatenate(
        [_rope(q_ref[:, r * HEAD_DIM:(r + 1) * HEAD_DIM], cos_q, sin_q).astype(BF16) for r in range(NSA_GROUP)],
        axis=0)
    sel = jnp.concatenate([sel_ref[...].astype(BF16)] * NSA_GROUP, axis=0)
    gate = jax.nn.sigmoid(gl_ref[...] + bg_ref[...])

    w0 = pl.multiple_of(jnp.maximum(q0 - WINDOW, 0), tq)
    t_w = q0 + (lax.broadcasted_iota(I32, (rows, span), 0) & (tq - 1))
    k_w = w0 + lax.broadcasted_iota(I32, (rows, span), 1)
    mask_w = (k_w <= t_w) & (t_w - k_w < WINDOW)
    s = jnp.where(mask_w, _dot_nt(q, kw_r[pl.ds(w0, span), :]) * scale, NEG_INF)
    o_w = _dot(_softmax_rows(s).astype(BF16), vw_b[pl.ds(w0, span), :])

    def selected_branch(n_keys):
        sel_keys = _dot(sel, ex_ref[:, :n_keys]) > 0.5
        t_s = q0 + (lax.broadcasted_iota(I32, (rows, n_keys), 0) & (tq - 1))
        k_s = lax.broadcasted_iota(I32, (rows, n_keys), 1)
        s = jnp.where(sel_keys & (k_s <= t_s), _dot_nt(q, ks_r[:n_keys, :]) * scale, NEG_INF)
        o_s = _dot(_softmax_rows(s).astype(BF16), vs_b[:n_keys, :])
        for r in range(NSA_GROUP):
            cols = slice(r * HEAD_DIM, (r + 1) * HEAD_DIM)
            part = slice(r * tq, (r + 1) * tq)
            o_ref[:, cols] = (gate[:, 3 * r:3 * r + 1] * oc_ref[:, cols]
                              + gate[:, 3 * r + 1:3 * r + 2] * o_s[part]
                              + gate[:, 3 * r + 2:3 * r + 3] * o_w[part])

    per_class = nq // SEL_PREFIX_CLASSES
    for c in range(SEL_PREFIX_CLASSES):
        @pl.when(qi // per_class == c)
        def _(c=c):
            selected_branch((c + 1) * per_class * tq)


def nsa_sel_win_attention(proj, b_gate2, o_cmp, sel, cos, sin, expand, B, S, tq):
    M = proj.shape[0]
    nq = S // tq
    G = NSA_KV_HEADS
    gw = NSA_GROUP * HEAD_DIM

    def kv_spec(cb):
        return pl.BlockSpec((S, HEAD_DIM), lambda b, g, i: (b, cb + g))

    return pl.pallas_call(
        functools.partial(_sel_win_kernel, tq),
        grid=(B, G, nq),
        in_specs=[pl.BlockSpec((tq, gw), lambda b, g, i: (b * nq + i, g)),
                  pl.BlockSpec((tq, LANES), lambda b, g, i: (b * nq + i, CB_GATE + g)),
                  pl.BlockSpec((None, 1, LANES), lambda b, g, i: (g, 0, 0)),
                  pl.BlockSpec((tq, gw), lambda b, g, i: (b * nq + i, g)),
                  pl.BlockSpec((None, tq, LANES), lambda b, g, i: (b * G + g, i, 0)),
                  kv_spec(CB_KS), kv_spec(CB_VS), kv_spec(CB_KW), kv_spec(CB_VW),
                  pl.BlockSpec((S, HEAD_DIM), lambda b, g, i: (0, 0)),
                  pl.BlockSpec((S, HEAD_DIM), lambda b, g, i: (0, 0)),
                  pl.BlockSpec(expand.shape, lambda b, g, i: (0, 0))],
        out_specs=pl.BlockSpec((tq, gw), lambda b, g, i: (b * nq + i, g)),
        out_shape=jax.ShapeDtypeStruct((M, NSA_HEADS * HEAD_DIM), F32),
        scratch_shapes=[pltpu.VMEM((S, HEAD_DIM), BF16)] * 4,
        compiler_params=_params("parallel", "parallel", "arbitrary"),
        name="nsa_sel_win_attention",
    )(proj, proj, b_gate2, o_cmp, sel, proj, proj, proj, proj, cos, sin, expand)


def _retention_kernel(q_ref, k_ref, v_ref, gate_ref, cos_ref, sin_ref, dm_ref, xi_ref, zeta_ref,
                      cdec_ref, gng_ref, gnb_ref, o_ref, state):
    @pl.when(pl.program_id(1) == 0)
    def _():
        state[...] = jnp.zeros_like(state)

    cos = cos_ref[...]
    sin = sin_ref[...]
    for h in range(RET_HEADS):
        cols = slice(h * HEAD_DIM, (h + 1) * HEAD_DIM)
        q = _rope(q_ref[:, cols], cos, sin)
        k = _rope(k_ref[:, cols], cos, sin) * (HEAD_DIM ** -0.5)
        qb = q.astype(BF16)
        vb = v_ref[:, cols].astype(BF16)
        inner = _dot_nt(qb, k.astype(BF16)) * dm_ref[h]
        r_old = state[h]
        o = _dot(inner.astype(BF16), vb) + _dot(qb, r_old.astype(BF16)) * xi_ref[h]
        state[h] = r_old * cdec_ref[h] + _dot_tn((k * zeta_ref[h]).astype(BF16), vb)
        mu = jnp.mean(o, axis=-1, keepdims=True)
        d = o - mu
        var = jnp.mean(d * d, axis=-1, keepdims=True)
        y = d * lax.rsqrt(var + GN_EPS) * gng_ref[:, cols] + gnb_ref[:, cols]
        gate = gate_ref[:, cols]
        o_ref[:, cols] = gate * jax.nn.sigmoid(gate) * y


def retention(proj, cos, sin, dmask, xi, zeta, cdec, gn_g, gn_b, B, S):
    M = proj.shape[0]
    C = RET_CHUNK
    nc = S // C
    H = RET_HEADS
    W = H * HEAD_DIM

    def col_spec(cb):
        return pl.BlockSpec((C, W), lambda b, c: (b * nc + c, cb // H))

    def table_spec(rows):
        return pl.BlockSpec((H, rows, HEAD_DIM), lambda b, c: (0, 0, 0))

    return pl.pallas_call(
        _retention_kernel,
        grid=(B, nc),
        in_specs=[col_spec(CB_RQ), col_spec(CB_RK), col_spec(CB_RV), col_spec(CB_RG),
                  pl.BlockSpec((C, HEAD_DIM), lambda b, c: (c, 0)),
                  pl.BlockSpec((C, HEAD_DIM), lambda b, c: (c, 0)),
                  table_spec(C), table_spec(C), table_spec(C), table_spec(1),
                  pl.BlockSpec((1, W), lambda b, c: (0, 0)),
                  pl.BlockSpec((1, W), lambda b, c: (0, 0))],
        out_specs=pl.BlockSpec((C, W), lambda b, c: (b * nc + c, 0)),
        out_shape=jax.ShapeDtypeStruct((M, W), F32),
        scratch_shapes=[pltpu.VMEM((H, HEAD_DIM, HEAD_DIM), F32)],
        compiler_params=_params("parallel", "arbitrary"),
        name="retention",
    )(proj, proj, proj, proj, cos, sin, dmask, xi, zeta, cdec, gn_g, gn_b)


def _xattn_kernel(q_ref, k_ref, v_ref, o_ref):
    hd = q_ref.shape[1] // XA_HEADS
    scale = hd ** -0.5
    for h in range(XA_HEADS):
        cols = slice(h * hd, (h + 1) * hd)
        s = _dot_nt(q_ref[:, cols].astype(BF16), k_ref[:, cols].astype(BF16)) * scale
        o_ref[:, cols] = _dot(_softmax_rows(s).astype(BF16), v_ref[:, cols].astype(BF16))


def cross_attention(q, kv, B, S, n_mem, tq):
    M, D = q.shape
    nq = S // tq
    return pl.pallas_call(
        _xattn_kernel,
        grid=(B, nq),
        in_specs=[pl.BlockSpec((tq, D), lambda b, i: (b * nq + i, 0)),
                  pl.BlockSpec((n_mem, D), lambda b, i: (b, 0)),
                  pl.BlockSpec((n_mem, D), lambda b, i: (b, 1))],
        out_specs=pl.BlockSpec((tq, D), lambda b, i: (b * nq + i, 0)),
        out_shape=jax.ShapeDtypeStruct((M, D), F32),
        compiler_params=_params("parallel", "parallel"),
        name="cross_attention",
    )(q, kv, kv)


def _top_k_rows(x, k, code):
    big = jnp.iinfo(jnp.int32).max
    vals, codes = [], []
    cur = x
    for _ in range(k):
        m = jnp.max(cur, axis=0, keepdims=True)
        c = jnp.min(jnp.where(cur == m, code, big), axis=0, keepdims=True)
        vals.append(m)
        codes.append(c)
        cur = jnp.where(code == c, -jnp.inf, cur)
    return jnp.concatenate(vals, axis=0), jnp.concatenate(codes, axis=0)


def _take_rows(table, idx, n):
    out = jnp.zeros(idx.shape, table.dtype)
    for p in range(n):
        out = jnp.where(idx == p, table[p:p + 1, :], out)
    return out


def _pair_candidates(s1, s2):
    K = PEER_TOPK
    tm = s1.shape[1]
    half = K // 2
    p_full = lax.broadcasted_iota(I32, (K, tm), 0)
    p_half = lax.broadcasted_iota(I32, (half, tm), 0)
    vals = [s1 + s2[0:1, :]]
    codes = [p_full * K]
    for q in range(1, half):
        n_valid = K // (q + 1)
        vals.append(jnp.where(p_half < n_valid, s1[:half, :] + s2[q:q + 1, :], -jnp.inf))
        codes.append(p_half * K + q)
    vals.append(s1[0:1, :] + s2[half:, :])
    codes.append(p_half + half)
    return jnp.concatenate(vals, axis=0), jnp.concatenate(codes, axis=0)


def _peer_topk_kernel(q_ref, keys_ref, a_ref, b_ref, g_ref):
    K = PEER_TOPK
    tm = q_ref.shape[0]
    key_row = lax.broadcasted_iota(I32, (PEER_NKEYS, tm), 0)
    for h in range(PEER_HEADS):
        halves = []
        for p in range(2):
            c0 = (2 * h + p) * PEER_HALF
            qhp = q_ref[:, c0:c0 + PEER_HALF].astype(BF16)
            s = _dot_nt(keys_ref[2 * h + p], qhp)
            halves.append(_top_k_rows(s, K, key_row))
        (s1, i1), (s2, i2) = halves
        cand, code = _pair_candidates(s1, s2)
        top, pos = _top_k_rows(cand, K, code)
        a_ref[h * K:(h + 1) * K, :] = _take_rows(i1, pos // K, K)
        b_ref[h * K:(h + 1) * K, :] = _take_rows(i2, pos % K, K)
        e = jnp.exp(top - jnp.max(top, axis=0, keepdims=True))
        g_ref[h * K:(h + 1) * K, :] = e / jnp.sum(e, axis=0, keepdims=True)


def peer_topk(q, keys, tm):
    T, D = q.shape
    HK = PEER_HEADS * PEER_TOPK
    out = jax.ShapeDtypeStruct((HK, T), I32)
    spec = pl.BlockSpec((HK, tm), lambda i: (0, i))
    return pl.pallas_call(
        _peer_topk_kernel,
        grid=(T // tm,),
        in_specs=[pl.BlockSpec((tm, D), lambda i: (i, 0)),
                  pl.BlockSpec(keys.shape, lambda i: (0, 0, 0))],
        out_specs=[spec, spec, spec],
        out_shape=[out, out, jax.ShapeDtypeStruct((HK, T), F32)],
        compiler_params=_params("parallel"),
        name="peer_topk",
    )(q, keys)


GATE_MAP_UNROLL = 8
GATE_MAP_TRIPS_UNROLLED = 4


def _peer_gate_map_kernel(tg, a_ref, b_ref, g_ref, o_ref):
    row = lax.broadcasted_iota(I32, (PEER_NKEYS, LANES), 0)

    def body(i, carry):
        t0 = pl.multiple_of(i * GATE_MAP_UNROLL, GATE_MAP_UNROLL)
        a8 = a_ref[pl.ds(t0, GATE_MAP_UNROLL), :]
        b8 = b_ref[pl.ds(t0, GATE_MAP_UNROLL), :]
        g8 = g_ref[pl.ds(t0, GATE_MAP_UNROLL), :]
        g8_hi = g8.astype(BF16).astype(F32)
        g8_lo = g8 - g8_hi
        maps = []
        for u in range(GATE_MAP_UNROLL):
            onehot_a = jnp.where(a8[u:u + 1, :] == row, 1.0, 0.0).astype(BF16)
            hit_b = b8[u:u + 1, :] == row
            b_hi = jnp.where(hit_b, g8_hi[u:u + 1, :], 0.0).astype(BF16)
            b_lo = jnp.where(hit_b, g8_lo[u:u + 1, :], 0.0).astype(BF16)
            maps.append(_dot_nt(jnp.concatenate([onehot_a, onehot_a], axis=1),
                                jnp.concatenate([b_hi, b_lo], axis=1)))
        o_ref[:, pl.ds(t0, GATE_MAP_UNROLL), :] = pltpu.einshape("trc->rtc", jnp.stack(maps, axis=0))
        return carry

    lax.fori_loop(0, tg // GATE_MAP_UNROLL, body, 0, unroll=GATE_MAP_TRIPS_UNROLLED)


def peer_gate_map(a, b, g, tg):
    T, HK = a.shape
    spec = pl.BlockSpec((tg, HK), lambda i: (i, 0))
    return pl.pallas_call(
        functools.partial(_peer_gate_map_kernel, tg),
        grid=(T // tg,),
        in_specs=[spec, spec, spec],
        out_specs=pl.BlockSpec((None, PEER_NKEYS, tg, PEER_NKEYS), lambda i: (i, 0, 0, 0)),
        out_shape=jax.ShapeDtypeStruct((T // tg, PEER_NKEYS, tg, PEER_NKEYS), F32),
        compiler_params=_params("parallel"),
        name="peer_gate_map",
    )(a, b, g)


def _peer_ffn_kernel(alpha, x_ref, u_ref, v_ref, gm_ref, lg_ref, lb_ref, o_ref, x_bf, h_scr, w_bf, acc):
    j = pl.program_id(1)
    n_tiles = pl.num_programs(1) - 1
    n_sub, nr, tg, _ = gm_ref.shape
    slot = j % 2

    def project(dst):
        h_scr[dst] = _dot_nt(x_bf[...], u_ref[...])

    def gate_and_accumulate(src):
        for r in range(nr):
            cols = slice(r * PEER_NKEYS, (r + 1) * PEER_NKEYS)
            for s in range(n_sub):
                rows = slice(s * tg, (s + 1) * tg)
                w_bf[rows, cols] = (gm_ref[s, r] * _gelu(h_scr[src, rows, cols])).astype(BF16)
        acc[...] += _dot(w_bf[...], v_ref[...])

    @pl.when(j == 0)
    def _():
        x_bf[...] = x_ref[...].astype(BF16)
        acc[...] = jnp.zeros_like(acc)
        project(0)

    @pl.when((j > 0) & (j < n_tiles))
    def _():
        project(slot)
        gate_and_accumulate(1 - slot)

    @pl.when(j == n_tiles)
    def _():
        gate_and_accumulate(1 - slot)
        o_ref[...] = _layer_norm(alpha * x_ref[...] + acc[...], lg_ref[...], lb_ref[...])


def peer_ffn(x, u, v, gate_map, ln_g, ln_b, alpha, tm, nr):
    T, D = x.shape
    NE = u.shape[0]
    te = nr * PEER_NKEYS
    tg = gate_map.shape[2]
    n_tiles = NE // te
    return pl.pallas_call(
        functools.partial(_peer_ffn_kernel, alpha),
        grid=(T // tm, n_tiles + 1),
        in_specs=[pl.BlockSpec((tm, D), lambda i, j: (i, 0)),
                  pl.BlockSpec((te, D), lambda i, j: (jnp.minimum(j, n_tiles - 1), 0)),
                  pl.BlockSpec((te, D), lambda i, j: (jnp.maximum(j - 1, 0), 0)),
                  pl.BlockSpec((tm // tg, nr, tg, PEER_NKEYS), lambda i, j: (i, jnp.maximum(j - 1, 0), 0, 0)),
                  pl.BlockSpec((1, D), lambda i, j: (0, 0)),
                  pl.BlockSpec((1, D), lambda i, j: (0, 0))],
        out_specs=pl.BlockSpec((tm, D), lambda i, j: (i, 0)),
        out_shape=jax.ShapeDtypeStruct((T, D), F32),
        scratch_shapes=[pltpu.VMEM((tm, D), BF16), pltpu.VMEM((2, tm, te), F32), pltpu.VMEM((tm, te), BF16),
                        pltpu.VMEM((tm, D), F32)],
        compiler_params=_params("parallel", "arbitrary"),
        name="peer_ffn",
    )(x, u, v, gate_map, ln_g.reshape(1, D), ln_b.reshape(1, D))


def _rope_tables(S):
    inv = 1.0 / (ROPE_THETA ** (jnp.arange(0, HEAD_DIM, 2, dtype=F32) / HEAD_DIM))
    ang = jnp.arange(S, dtype=F32)[:, None] * inv[None, :]
    cos = jnp.cos(ang)
    sin = jnp.sin(ang)
    return jnp.concatenate([cos, cos], -1), jnp.concatenate([-sin, sin], -1)


def _retention_tables():
    H, C = RET_HEADS, RET_CHUNK
    log_g = jnp.log(1.0 - 2.0 ** (-5.0 - jnp.arange(H, dtype=F32)))
    i = jnp.arange(C, dtype=F32)
    diff = i[:, None] - i[None, :]
    causal = diff >= 0
    dmask = jnp.where(causal[None], jnp.exp(jnp.where(causal, diff, 0.0)[None] * log_g[:, None, None]), 0.0)
    xi = jnp.exp((i[None, :] + 1.0) * log_g[:, None])
    zeta = jnp.exp((C - 1.0 - i[None, :]) * log_g[:, None])
    cdec = jnp.exp(C * log_g)
    wide = (H, C, HEAD_DIM)
    return (dmask, jnp.broadcast_to(xi[:, :, None], wide), jnp.broadcast_to(zeta[:, :, None], wide),
            jnp.broadcast_to(cdec[:, None, None], (H, 1, HEAD_DIM)))


def _overlap_matrix(n_blk, S):
    n = np.arange(n_blk)[:, None] * CMP_STRIDE
    j = np.arange(LANES)[None, :] * SLC_BLOCK
    n_cmp = (S - CMP_BLOCK) // CMP_STRIDE + 1
    ov = (n < j + SLC_BLOCK) & (n + CMP_BLOCK > j) & (np.arange(n_blk)[:, None] < n_cmp) & (j < S)
    return jnp.asarray(ov, BF16)


def _expand_matrix(S):
    ex = (np.arange(S)[None, :] // SLC_BLOCK) == np.arange(LANES)[:, None]
    return jnp.asarray(ex, BF16)


def _pack_w_in(w_in, b_gate):
    D = w_in.shape[0]
    nsa_w = NSA_HEADS * HEAD_DIM
    kv_w = 6 * NSA_KV_HEADS * HEAD_DIM
    n_gate = 3 * NSA_HEADS
    per_group = n_gate // NSA_KV_HEADS
    gate_w = w_in[:, nsa_w + kv_w:nsa_w + kv_w + n_gate].reshape(D, NSA_KV_HEADS, per_group)
    gate_w = jnp.pad(gate_w, ((0, 0), (0, 0), (0, LANES - per_group))).reshape(D, NSA_KV_HEADS * LANES)
    pad_w = jnp.zeros((D, (CB_RQ - CB_PAD) * LANES), w_in.dtype)
    packed = jnp.concatenate([w_in[:, :nsa_w + kv_w], gate_w, pad_w, w_in[:, nsa_w + kv_w + n_gate:]], axis=1)
    bg = jnp.pad(b_gate.reshape(NSA_KV_HEADS, 1, per_group), ((0, 0), (0, 0), (0, LANES - per_group)))
    return packed.astype(BF16), bg


def _mixer(x2, B, S, w_in, b_gate, cmp_pos, cmp_w1, cmp_w2, gn_g, gn_b, w_out, ln_g, ln_b, alpha, tables):
    cos, sin, ret_tabs, expand = tables
    G = NSA_KV_HEADS
    w_packed, bg = _pack_w_in(w_in, b_gate)
    proj = matmul(x2, w_packed, tm=1024, tn=1024)
    n_blk = S // CMP_STRIDE
    kv = proj[:, CB_KC * LANES:CB_KS * LANES].reshape(B, n_blk, CMP_STRIDE, 2, G, HEAD_DIM)
    x16 = kv.transpose(3, 0, 4, 1, 2, 5).reshape(2, B * G, n_blk, CMP_STRIDE * HEAD_DIM)
    kv_cmp = nsa_compress(x16, cmp_pos.reshape(2, 1, CMP_BLOCK * HEAD_DIM), cmp_w1.astype(BF16),
                          cmp_w2.astype(BF16))
    o_cmp, sel = nsa_cmp_attention(proj, kv_cmp, _overlap_matrix(n_blk, S), B, S, tq=256)
    o_nsa = nsa_sel_win_attention(proj, bg, o_cmp, sel, cos, sin, expand, B, S, tq=128)
    o_ret = retention(proj, cos, sin, *ret_tabs, gn_g.reshape(1, -1), gn_b.reshape(1, -1), B, S)
    return matmul_residual_ln([o_nsa, o_ret], w_out.astype(BF16), x2, ln_g, ln_b, alpha, tm=512)


def _cross(x2, mem2, B, S, wq, wk, wv, wo, ln_g, ln_b, alpha):
    n_mem = mem2.shape[0] // B
    q = matmul(x2, wq.astype(BF16), tm=1024, tn=1024)
    kv = matmul(mem2, jnp.concatenate([wk, wv], axis=1).astype(BF16), tm=mem2.shape[0], tn=1024)
    o = cross_attention(q, kv, B, S, n_mem, tq=256)
    return matmul_residual_ln([o], wo.astype(BF16), x2, ln_g, ln_b, alpha, tm=512)


def _peer(x2, w_q, sub_keys, u_tab, v_tab, ln_g, ln_b, alpha):
    q = matmul(x2, w_q.astype(BF16), tm=1024, tn=1024)
    keys = sub_keys.reshape(PEER_HEADS * 2, PEER_NKEYS, PEER_HALF).astype(BF16)
    a, b, g = peer_topk(q, keys, tm=256)
    gate_map = peer_gate_map(a.T, b.T, g.T, tg=128)
    return peer_ffn(x2, u_tab.astype(BF16), v_tab.astype(BF16), gate_map, ln_g, ln_b, alpha, tm=512, nr=8)


def kernel(x, mem, w_in, b_gate, cmp_pos, cmp_w1, cmp_w2, ret_gn_g, ret_gn_b, w_mix_out, ln1_g, ln1_b,
           xa_wq, xa_wk, xa_wv, xa_wo, ln2_g, ln2_b, peer_wq, peer_sub_keys, peer_u, peer_v, ln3_g, ln3_b):
    B, S, D = x.shape
    depth = w_in.shape[0]
    alpha = (2 * depth) ** 0.25
    tables = (*_rope_tables(S), _retention_tables(), _expand_matrix(S))
    x2 = x.reshape(B * S, D)
    mem2 = mem.reshape(-1, D)
    for l in range(depth):
        x2 = _mixer(x2, B, S, w_in[l], b_gate[l], cmp_pos[l], cmp_w1[l], cmp_w2[l], ret_gn_g[l], ret_gn_b[l],
                    cast_layer_bf16(w_mix_out, l), ln1_g[l], ln1_b[l], alpha, tables)
        x2 = _cross(x2, mem2, B, S, cast_layer_bf16(xa_wq, l), xa_wk[l], xa_wv[l], cast_layer_bf16(xa_wo, l),
                    ln2_g[l], ln2_b[l], alpha)
        x2 = _peer(x2, cast_layer_bf16(peer_wq, l), peer_sub_keys[l], cast_layer_bf16(peer_u, l),
                   cast_layer_bf16(peer_v, l), ln3_g[l], ln3_b[l], alpha)
    return x2.reshape(B, S, D)
```

```python
import functools

import numpy as np
import jax
import jax.numpy as jnp
from jax import lax
from jax.experimental import pallas as pl
from jax.experimental.pallas import tpu as pltpu

F32 = jnp.float32
BF16 = jnp.bfloat16
I32 = jnp.int32

HEAD_DIM = 128
LANES = 128
NSA_HEADS = 8
NSA_KV_HEADS = 2
NSA_GROUP = NSA_HEADS // NSA_KV_HEADS
RET_HEADS = 8
CMP_BLOCK = 32
CMP_STRIDE = 16
SLC_BLOCK = 64
SLC_TOPK = 16
WINDOW = 512
RET_CHUNK = 128
XA_HEADS = 4
PEER_HEADS = 8
PEER_NKEYS = 128
PEER_HALF = 128
PEER_TOPK = 16
ROPE_THETA = 10000.0
LN_EPS = 1e-5
GN_EPS = 1e-5
NEG_INF = -1e30
FORCE_SCORE = 1e9
LOG2_E = 1.4426950408889634
VMEM_LIMIT = 56 * 1024 * 1024

CB_Q = 0
CB_KC, CB_VC, CB_KS, CB_VS, CB_KW, CB_VW = 8, 10, 12, 14, 16, 18
CB_GATE = 20
CB_PAD = 22
CB_RQ, CB_RK, CB_RV, CB_RG = 24, 32, 40, 48
P_PACKED = 56 * LANES


def _dot(a, b):
    return jnp.dot(a, b, preferred_element_type=F32)


def _dot_nt(a, b):
    return lax.dot_general(a, b, (((1,), (1,)), ((), ())), preferred_element_type=F32)


def _dot_tn(a, b):
    return lax.dot_general(a, b, (((0,), (0,)), ((), ())), preferred_element_type=F32)


def _gelu(x):
    return 0.5 * x * (1.0 + lax.erf(x * (2.0 ** -0.5)))


def _rope(x, cos, sin_signed):
    return x * cos + pltpu.roll(x, HEAD_DIM // 2, axis=1) * sin_signed


def _layer_norm(y, g, b):
    mu = jnp.mean(y, axis=-1, keepdims=True)
    d = y - mu
    var = jnp.mean(d * d, axis=-1, keepdims=True)
    return d * lax.rsqrt(var + LN_EPS) * g + b


def _softmax_pv(s, v):
    m = jnp.max(s, axis=-1, keepdims=True)
    e = jnp.exp(s - m)
    return _dot(e.astype(BF16), v) / jnp.sum(e, axis=-1, keepdims=True)


def _scaled_softmax_pv(s, scale, v):
    m = jnp.max(s, axis=-1, keepdims=True)
    e = jnp.exp2((s - m) * (scale * LOG2_E))
    return _dot(e.astype(BF16), v) / jnp.sum(e, axis=-1, keepdims=True)


def _params(*sem):
    return pltpu.CompilerParams(dimension_semantics=sem, vmem_limit_bytes=VMEM_LIMIT)


CAST_ROWS = 1024


def _cast_kernel(x_ref, o_ref):
    o_ref[...] = x_ref[...].astype(o_ref.dtype)


def cast_layer_bf16(stacked, l):
    _, R, C = stacked.shape
    tr = min(CAST_ROWS, R)
    return pl.pallas_call(
        _cast_kernel,
        grid=(R // tr,),
        in_specs=[pl.BlockSpec((None, tr, C), lambda i: (l, i, 0))],
        out_specs=pl.BlockSpec((tr, C), lambda i: (i, 0)),
        out_shape=jax.ShapeDtypeStruct((R, C), BF16),
        compiler_params=_params("parallel"),
        name="cast_bf16",
    )(stacked)


def _mm_kernel(a_ref, w_ref, o_ref, a_bf):
    @pl.when(pl.program_id(1) == 0)
    def _():
        a_bf[...] = a_ref[...].astype(BF16)

    o_ref[...] = _dot(a_bf[...], w_ref[...]).astype(o_ref.dtype)


def matmul(a, w, tm, tn):
    M, K = a.shape
    N = w.shape[1]
    assert M % tm == 0 and N % tn == 0
    return pl.pallas_call(
        _mm_kernel,
        grid=(M // tm, N // tn),
        in_specs=[pl.BlockSpec((tm, K), lambda i, j: (i, 0)),
                  pl.BlockSpec((K, tn), lambda i, j: (0, j))],
        out_specs=pl.BlockSpec((tm, tn), lambda i, j: (i, j)),
        out_shape=jax.ShapeDtypeStruct((M, N), F32),
        scratch_shapes=[pltpu.VMEM((tm, K), BF16)],
        compiler_params=_params("parallel", "arbitrary"),
        name="matmul",
    )(a, w)


def _mm_ln_kernel(n_in, alpha, *refs):
    a_refs = refs[:n_in]
    w_ref, x_ref, g_ref, b_ref, o_ref = refs[n_in:]
    acc = None
    off = 0
    for a_ref in a_refs:
        k = a_ref.shape[1]
        part = _dot(a_ref[...].astype(BF16), w_ref[off:off + k, :])
        acc = part if acc is None else acc + part
        off += k
    o_ref[...] = _layer_norm(alpha * x_ref[...] + acc, g_ref[...], b_ref[...])


def matmul_residual_ln(a_list, w, x, g, b, alpha, tm):
    M, D = x.shape
    n_in = len(a_list)
    in_specs = [pl.BlockSpec((tm, a.shape[1]), lambda i: (i, 0)) for a in a_list]
    in_specs += [pl.BlockSpec(w.shape, lambda i: (0, 0)),
                 pl.BlockSpec((tm, D), lambda i: (i, 0)),
                 pl.BlockSpec((1, D), lambda i: (0, 0)),
                 pl.BlockSpec((1, D), lambda i: (0, 0))]
    return pl.pallas_call(
        functools.partial(_mm_ln_kernel, n_in, alpha),
        grid=(M // tm,),
        in_specs=in_specs,
        out_specs=pl.BlockSpec((tm, D), lambda i: (i, 0)),
        out_shape=jax.ShapeDtypeStruct((M, D), F32),
        compiler_params=_params("parallel"),
        name="matmul_residual_ln",
    )(*a_list, w, x, g.reshape(1, D), b.reshape(1, D))


def _compress_kernel(x_ref, pos_ref, w1_ref, w2_ref, o_ref):
    x = x_ref[...]
    half = x.shape[1]
    pos = pos_ref[...]
    lo = (x + pos[:, :half]).astype(BF16)
    hi = (x + pos[:, half:]).astype(BF16)
    p_lo = _dot(lo, w1_ref[:half, :])
    p_hi = _dot(hi, w1_ref[half:, :])
    h = p_lo + pltpu.roll(p_hi, x.shape[0] - 1, axis=0)
    o_ref[...] = _dot(_gelu(h).astype(BF16), w2_ref[...])


def nsa_compress(x16, pos, w1, w2):
    _, BG, n_blk, wd = x16.shape
    return pl.pallas_call(
        _compress_kernel,
        grid=(2, BG),
        in_specs=[pl.BlockSpec((None, None, n_blk, wd), lambda i, j: (i, j, 0, 0)),
                  pl.BlockSpec((None, 1, 2 * wd), lambda i, j: (i, 0, 0)),
                  pl.BlockSpec((None, 2 * wd, HEAD_DIM), lambda i, j: (i, 0, 0)),
                  pl.BlockSpec((None, HEAD_DIM, HEAD_DIM), lambda i, j: (i, 0, 0))],
        out_specs=pl.BlockSpec((None, None, n_blk, HEAD_DIM), lambda i, j: (i, j, 0, 0)),
        out_shape=jax.ShapeDtypeStruct((2, BG, n_blk, HEAD_DIM), F32),
        compiler_params=_params("parallel", "parallel"),
        name="nsa_compress",
    )(x16, pos, w1, w2)


def _cmp_attn_kernel(tq, q_ref, kc_ref, vc_ref, ov_ref, o_ref, sel_ref):
    qi = pl.program_id(2)
    scale = HEAD_DIM ** -0.5
    t = qi * tq + lax.broadcasted_iota(I32, (tq, LANES), 0)
    lane = lax.broadcasted_iota(I32, (tq, LANES), 1)
    vis = (lane * CMP_STRIDE + (CMP_BLOCK - 1)) <= t
    kc = kc_ref[...].astype(BF16)
    vc = vc_ref[...].astype(BF16)
    psum = jnp.zeros((tq, LANES), F32)
    for r in range(NSA_GROUP):
        q = q_ref[:, r * HEAD_DIM:(r + 1) * HEAD_DIM].astype(BF16)
        s = _dot_nt(q, kc) * scale
        s = jnp.where(vis, s, NEG_INF)
        m = jnp.max(s, axis=-1, keepdims=True)
        e = jnp.where(vis, jnp.exp(s - m), 0.0)
        l = jnp.sum(e, axis=-1, keepdims=True)
        p = e / jnp.where(l > 0.0, l, 1.0)
        o_ref[:, r * HEAD_DIM:(r + 1) * HEAD_DIM] = _dot(p.astype(BF16), vc)
        psum = psum + p
    p_hi = psum.astype(BF16)
    p_lo = (psum - p_hi.astype(F32)).astype(BF16)
    imp = _dot(p_hi, ov_ref[...]) + _dot(p_lo, ov_ref[...])
    n_slc = ov_ref.shape[0] * CMP_STRIDE // SLC_BLOCK
    cur = t // SLC_BLOCK
    forced = (lane == 0) | (lane == cur) | (lane == cur - 1)
    score = jnp.where(forced, FORCE_SCORE, imp)
    score = jnp.where(lane * SLC_BLOCK <= t, score, -1.0)
    score = jnp.where(lane < n_slc, score, -2.0)
    cnt = jnp.zeros((tq, LANES), I32)
    for i in range(n_slc):
        ci = score[:, i:i + 1]
        before = (ci > score) | ((ci == score) & (lane > i))
        cnt = cnt + before.astype(I32)
    sel = (cnt < min(SLC_TOPK, n_slc)) & (lane < n_slc)
    sel_ref[...] = sel.astype(F32)


def nsa_cmp_attention(proj, kv_cmp, overlap, B, S, tq):
    M = proj.shape[0]
    nq = S // tq
    G = NSA_KV_HEADS
    gw = NSA_GROUP * HEAD_DIM
    n_blk = kv_cmp.shape[2]
    return pl.pallas_call(
        functools.partial(_cmp_attn_kernel, tq),
        grid=(B, G, nq),
        in_specs=[pl.BlockSpec((tq, gw), lambda b, g, i: (b * nq + i, g)),
                  pl.BlockSpec((None, None, n_blk, HEAD_DIM), lambda b, g, i: (0, b * G + g, 0, 0)),
                  pl.BlockSpec((None, None, n_blk, HEAD_DIM), lambda b, g, i: (1, b * G + g, 0, 0)),
                  pl.BlockSpec(overlap.shape, lambda b, g, i: (0, 0))],
        out_specs=[pl.BlockSpec((tq, gw), lambda b, g, i: (b * nq + i, g)),
                   pl.BlockSpec((None, tq, LANES), lambda b, g, i: (b * G + g, i, 0))],
        out_shape=[jax.ShapeDtypeStruct((M, NSA_HEADS * HEAD_DIM), F32),
                   jax.ShapeDtypeStruct((B * G, S, LANES), F32)],
        compiler_params=_params("parallel", "parallel", "parallel"),
        name="nsa_cmp_attention",
    )(proj, kv_cmp, kv_cmp, overlap)


SEL_PREFIX_CLASSES = 8


def _sel_win_kernel(tq, q_ref, gl_ref, bg_ref, oc_ref, sel_ref, ks_ref, vs_ref, kw_ref, vw_ref,
                    cos_ref, sin_ref, ex_ref, o_ref, ks_r, kw_r, vs_b, vw_b):
    qi = pl.program_id(2)
    nq = pl.num_programs(2)
    S = ks_ref.shape[0]
    scale = HEAD_DIM ** -0.5
    span = WINDOW + tq
    rows = NSA_GROUP * tq

    @pl.when(qi == 0)
    def _():
        cos = cos_ref[...]
        sin = sin_ref[...]
        ks_r[...] = _rope(ks_ref[...], cos, sin).astype(BF16)
        kw_r[...] = _rope(kw_ref[...], cos, sin).astype(BF16)
        vs_b[...] = vs_ref[...].astype(BF16)
        vw_b[...] = vw_ref[...].astype(BF16)

    q0 = pl.multiple_of(qi * tq, tq)
    cos_q = cos_ref[pl.ds(q0, tq), :]
    sin_q = sin_ref[pl.ds(q0, tq), :]
    q = jnp.concatenate(
        [_rope(q_ref[:, r * HEAD_DIM:(r + 1) * HEAD_DIM], cos_q, sin_q).astype(BF16) for r in range(NSA_GROUP)],
        axis=0)
    sel = jnp.concatenate([sel_ref[...].astype(BF16)] * NSA_GROUP, axis=0)
    gate = jax.nn.sigmoid(gl_ref[...] + bg_ref[...])

    w0 = pl.multiple_of(jnp.maximum(q0 - WINDOW, 0), tq)
    t_w = q0 + (lax.broadcasted_iota(I32, (rows, span), 0) & (tq - 1))
    k_w = w0 + lax.broadcasted_iota(I32, (rows, span), 1)
    mask_w = (k_w <= t_w) & (t_w - k_w < WINDOW)
    s = jnp.where(mask_w, _dot_nt(q, kw_r[pl.ds(w0, span), :]), NEG_INF)
    o_w = _scaled_softmax_pv(s, scale, vw_b[pl.ds(w0, span), :])

    def selected_branch(n_keys):
        sel_keys = _dot(sel, ex_ref[:, :n_keys]) > 0.5
        t_s = q0 + (lax.broadcasted_iota(I32, (rows, n_keys), 0) & (tq - 1))
        k_s = lax.broadcasted_iota(I32, (rows, n_keys), 1)
        s = jnp.where(sel_keys & (k_s <= t_s), _dot_nt(q, ks_r[:n_keys, :]), NEG_INF)
        o_s = _scaled_softmax_pv(s, scale, vs_b[:n_keys, :])
        for r in range(NSA_GROUP):
            cols = slice(r * HEAD_DIM, (r + 1) * HEAD_DIM)
            part = slice(r * tq, (r + 1) * tq)
            o_ref[:, cols] = (gate[:, 3 * r:3 * r + 1] * oc_ref[:, cols]
                              + gate[:, 3 * r + 1:3 * r + 2] * o_s[part]
                              + gate[:, 3 * r + 2:3 * r + 3] * o_w[part])

    per_class = nq // SEL_PREFIX_CLASSES
    for c in range(SEL_PREFIX_CLASSES):
        @pl.when(qi // per_class == c)
        def _(c=c):
            selected_branch((c + 1) * per_class * tq)


def nsa_sel_win_attention(proj, b_gate2, o_cmp, sel, cos, sin, expand, B, S, tq):
    M = proj.shape[0]
    nq = S // tq
    G = NSA_KV_HEADS
    gw = NSA_GROUP * HEAD_DIM

    def kv_spec(cb):
        return pl.BlockSpec((S, HEAD_DIM), lambda b, g, i: (b, cb + g))

    return pl.pallas_call(
        functools.partial(_sel_win_kernel, tq),
        grid=(B, G, nq),
        in_specs=[pl.BlockSpec((tq, gw), lambda b, g, i: (b * nq + i, g)),
                  pl.BlockSpec((tq, LANES), lambda b, g, i: (b * nq + i, CB_GATE + g)),
                  pl.BlockSpec((None, 1, LANES), lambda b, g, i: (g, 0, 0)),
                  pl.BlockSpec((tq, gw), lambda b, g, i: (b * nq + i, g)),
                  pl.BlockSpec((None, tq, LANES), lambda b, g, i: (b * G + g, i, 0)),
                  kv_spec(CB_KS), kv_spec(CB_VS), kv_spec(CB_KW), kv_spec(CB_VW),
                  pl.BlockSpec((S, HEAD_DIM), lambda b, g, i: (0, 0)),
                  pl.BlockSpec((S, HEAD_DIM), lambda b, g, i: (0, 0)),
                  pl.BlockSpec(expand.shape, lambda b, g, i: (0, 0))],
        out_specs=pl.BlockSpec((tq, gw), lambda b, g, i: (b * nq + i, g)),
        out_shape=jax.ShapeDtypeStruct((M, NSA_HEADS * HEAD_DIM), F32),
        scratch_shapes=[pltpu.VMEM((S, HEAD_DIM), BF16)] * 4,
        compiler_params=_params("parallel", "parallel", "arbitrary"),
        name="nsa_sel_win_attention",
    )(proj, proj, b_gate2, o_cmp, sel, proj, proj, proj, proj, cos, sin, expand)


def _retention_kernel(q_ref, k_ref, v_ref, gate_ref, cos_ref, sin_ref, dm_ref, xi_ref, zeta_ref,
                      cdec_ref, gng_ref, gnb_ref, o_ref, state):
    @pl.when(pl.program_id(1) == 0)
    def _():
        state[...] = jnp.zeros_like(state)

    cos = cos_ref[...]
    sin = sin_ref[...]
    for h in range(RET_HEADS):
        cols = slice(h * HEAD_DIM, (h + 1) * HEAD_DIM)
        q = _rope(q_ref[:, cols], cos, sin)
        k = _rope(k_ref[:, cols], cos, sin) * (HEAD_DIM ** -0.5)
        qb = q.astype(BF16)
        vb = v_ref[:, cols].astype(BF16)
        inner = _dot_nt(qb, k.astype(BF16)) * dm_ref[h]
        r_old = state[h]
        o = _dot(inner.astype(BF16), vb) + _dot(qb, r_old.astype(BF16)) * xi_ref[h]
        state[h] = r_old * cdec_ref[h] + _dot_tn((k * zeta_ref[h]).astype(BF16), vb)
        mu = jnp.mean(o, axis=-1, keepdims=True)
        d = o - mu
        var = jnp.mean(d * d, axis=-1, keepdims=True)
        y = d * lax.rsqrt(var + GN_EPS) * gng_ref[:, cols] + gnb_ref[:, cols]
        gate = gate_ref[:, cols]
        o_ref[:, cols] = gate * jax.nn.sigmoid(gate) * y


def retention(proj, cos, sin, dmask, xi, zeta, cdec, gn_g, gn_b, B, S):
    M = proj.shape[0]
    C = RET_CHUNK
    nc = S // C
    H = RET_HEADS
    W = H * HEAD_DIM

    def col_spec(cb):
        return pl.BlockSpec((C, W), lambda b, c: (b * nc + c, cb // H))

    def table_spec(rows):
        return pl.BlockSpec((H, rows, HEAD_DIM), lambda b, c: (0, 0, 0))

    return pl.pallas_call(
        _retention_kernel,
        grid=(B, nc),
        in_specs=[col_spec(CB_RQ), col_spec(CB_RK), col_spec(CB_RV), col_spec(CB_RG),
                  pl.BlockSpec((C, HEAD_DIM), lambda b, c: (c, 0)),
                  pl.BlockSpec((C, HEAD_DIM), lambda b, c: (c, 0)),
                  table_spec(C), table_spec(C), table_spec(C), table_spec(1),
                  pl.BlockSpec((1, W), lambda b, c: (0, 0)),
                  pl.BlockSpec((1, W), lambda b, c: (0, 0))],
        out_specs=pl.BlockSpec((C, W), lambda b, c: (b * nc + c, 0)),
        out_shape=jax.ShapeDtypeStruct((M, W), F32),
        scratch_shapes=[pltpu.VMEM((H, HEAD_DIM, HEAD_DIM), F32)],
        compiler_params=_params("parallel", "arbitrary"),
        name="retention",
    )(proj, proj, proj, proj, cos, sin, dmask, xi, zeta, cdec, gn_g, gn_b)


def _xattn_kernel(q_ref, k_ref, v_ref, o_ref):
    hd = q_ref.shape[1] // XA_HEADS
    scale = hd ** -0.5
    for h in range(XA_HEADS):
        cols = slice(h * hd, (h + 1) * hd)
        s = _dot_nt(q_ref[:, cols].astype(BF16), k_ref[:, cols].astype(BF16)) * scale
        o_ref[:, cols] = _softmax_pv(s, v_ref[:, cols].astype(BF16))


def cross_attention(q, kv, B, S, n_mem, tq):
    M, D = q.shape
    nq = S // tq
    return pl.pallas_call(
        _xattn_kernel,
        grid=(B, nq),
        in_specs=[pl.BlockSpec((tq, D), lambda b, i: (b * nq + i, 0)),
                  pl.BlockSpec((n_mem, D), lambda b, i: (b, 0)),
                  pl.BlockSpec((n_mem, D), lambda b, i: (b, 1))],
        out_specs=pl.BlockSpec((tq, D), lambda b, i: (b * nq + i, 0)),
        out_shape=jax.ShapeDtypeStruct((M, D), F32),
        compiler_params=_params("parallel", "parallel"),
        name="cross_attention",
    )(q, kv, kv)


def _top_k_rows(x, k, code):
    big = jnp.iinfo(jnp.int32).max
    vals, codes = [], []
    cur = x
    for _ in range(k):
        m = jnp.max(cur, axis=0, keepdims=True)
        c = jnp.min(jnp.where(cur == m, code, big), axis=0, keepdims=True)
        vals.append(m)
        codes.append(c)
        cur = jnp.where(code == c, -jnp.inf, cur)
    return jnp.concatenate(vals, axis=0), jnp.concatenate(codes, axis=0)


def _take_rows(table, idx, n):
    out = jnp.zeros(idx.shape, table.dtype)
    for p in range(n):
        out = jnp.where(idx == p, table[p:p + 1, :], out)
    return out


def _pair_candidates(s1, s2):
    K = PEER_TOPK
    tm = s1.shape[1]
    half = K // 2
    p_full = lax.broadcasted_iota(I32, (K, tm), 0)
    p_half = lax.broadcasted_iota(I32, (half, tm), 0)
    vals = [s1 + s2[0:1, :]]
    codes = [p_full * K]
    for q in range(1, half):
        n_valid = K // (q + 1)
        vals.append(jnp.where(p_half < n_valid, s1[:half, :] + s2[q:q + 1, :], -jnp.inf))
        codes.append(p_half * K + q)
    vals.append(s1[0:1, :] + s2[half:, :])
    codes.append(p_half + half)
    return jnp.concatenate(vals, axis=0), jnp.concatenate(codes, axis=0)


def _peer_topk_kernel(q_ref, keys_ref, a_ref, b_ref, g_ref):
    K = PEER_TOPK
    tm = q_ref.shape[0]
    key_row = lax.broadcasted_iota(I32, (PEER_NKEYS, tm), 0)
    for h in range(PEER_HEADS):
        halves = []
        for p in range(2):
            c0 = (2 * h + p) * PEER_HALF
            qhp = q_ref[:, c0:c0 + PEER_HALF].astype(BF16)
            s = _dot_nt(keys_ref[2 * h + p], qhp)
            halves.append(_top_k_rows(s, K, key_row))
        (s1, i1), (s2, i2) = halves
        cand, code = _pair_candidates(s1, s2)
        top, pos = _top_k_rows(cand, K, code)
        a_ref[h * K:(h + 1) * K, :] = _take_rows(i1, pos // K, K)
        b_ref[h * K:(h + 1) * K, :] = _take_rows(i2, pos % K, K)
        e = jnp.exp(top - jnp.max(top, axis=0, keepdims=True))
        g_ref[h * K:(h + 1) * K, :] = e / jnp.sum(e, axis=0, keepdims=True)


def peer_topk(q, keys, tm):
    T, D = q.shape
    HK = PEER_HEADS * PEER_TOPK
    out = jax.ShapeDtypeStruct((HK, T), I32)
    spec = pl.BlockSpec((HK, tm), lambda i: (0, i))
    return pl.pallas_call(
        _peer_topk_kernel,
        grid=(T // tm,),
        in_specs=[pl.BlockSpec((tm, D), lambda i: (i, 0)),
                  pl.BlockSpec(keys.shape, lambda i: (0, 0, 0))],
        out_specs=[spec, spec, spec],
        out_shape=[out, out, jax.ShapeDtypeStruct((HK, T), F32)],
        compiler_params=_params("parallel"),
        name="peer_topk",
    )(q, keys)


GATE_MAP_UNROLL = 8
GATE_MAP_TRIPS_UNROLLED = 4


def _peer_gate_map_kernel(tg, a_ref, b_ref, g_ref, o_ref):
    row = lax.broadcasted_iota(I32, (PEER_NKEYS, LANES), 0)

    def body(i, carry):
        t0 = pl.multiple_of(i * GATE_MAP_UNROLL, GATE_MAP_UNROLL)
        a8 = a_ref[pl.ds(t0, GATE_MAP_UNROLL), :]
        b8 = b_ref[pl.ds(t0, GATE_MAP_UNROLL), :]
        g8 = g_ref[pl.ds(t0, GATE_MAP_UNROLL), :]
        g8_hi = g8.astype(BF16).astype(F32)
        g8_lo = g8 - g8_hi
        maps = []
        for u in range(GATE_MAP_UNROLL):
            onehot_a = jnp.where(a8[u:u + 1, :] == row, 1.0, 0.0).astype(BF16)
            hit_b = b8[u:u + 1, :] == row
            b_hi = jnp.where(hit_b, g8_hi[u:u + 1, :], 0.0).astype(BF16)
            b_lo = jnp.where(hit_b, g8_lo[u:u + 1, :], 0.0).astype(BF16)
            maps.append(_dot_nt(jnp.concatenate([onehot_a, onehot_a], axis=1),
                                jnp.concatenate([b_hi, b_lo], axis=1)))
        o_ref[:, pl.ds(t0, GATE_MAP_UNROLL), :] = jnp.transpose(jnp.stack(maps, axis=0), (1, 0, 2))
        return carry

    lax.fori_loop(0, tg // GATE_MAP_UNROLL, body, 0, unroll=GATE_MAP_TRIPS_UNROLLED)


def peer_gate_map(a, b, g, tg):
    T, HK = a.shape
    spec = pl.BlockSpec((tg, HK), lambda i: (i, 0))
    return pl.pallas_call(
        functools.partial(_peer_gate_map_kernel, tg),
        grid=(T // tg,),
        in_specs=[spec, spec, spec],
        out_specs=pl.BlockSpec((None, PEER_NKEYS, tg, PEER_NKEYS), lambda i: (i, 0, 0, 0)),
        out_shape=jax.ShapeDtypeStruct((T // tg, PEER_NKEYS, tg, PEER_NKEYS), F32),
        compiler_params=_params("parallel"),
        name="peer_gate_map",
    )(a, b, g)


def _peer_ffn_kernel(alpha, x_ref, u_ref, v_ref, gm_ref, lg_ref, lb_ref, o_ref, x_bf, h_scr, w_bf, acc):
    j = pl.program_id(1)
    n_tiles = pl.num_programs(1) - 1
    n_sub, nr, tg, _ = gm_ref.shape
    slot = j % 2

    def project(dst):
        h_scr[dst] = _dot_nt(x_bf[...], u_ref[...])

    def gate_and_accumulate(src):
        for r in range(nr):
            cols = slice(r * PEER_NKEYS, (r + 1) * PEER_NKEYS)
            for s in range(n_sub):
                rows = slice(s * tg, (s + 1) * tg)
                w_bf[rows, cols] = (gm_ref[s, r] * _gelu(h_scr[src, rows, cols])).astype(BF16)
        acc[...] += _dot(w_bf[...], v_ref[...])

    @pl.when(j == 0)
    def _():
        x_bf[...] = x_ref[...].astype(BF16)
        acc[...] = jnp.zeros_like(acc)
        project(0)

    @pl.when((j > 0) & (j < n_tiles))
    def _():
        project(slot)
        gate_and_accumulate(1 - slot)

    @pl.when(j == n_tiles)
    def _():
        gate_and_accumulate(1 - slot)
        o_ref[...] = _layer_norm(alpha * x_ref[...] + acc[...], lg_ref[...], lb_ref[...])


def peer_ffn(x, u, v, gate_map, ln_g, ln_b, alpha, tm, nr):
    T, D = x.shape
    NE = u.shape[0]
    te = nr * PEER_NKEYS
    tg = gate_map.shape[2]
    n_tiles = NE // te
    return pl.pallas_call(
        functools.partial(_peer_ffn_kernel, alpha),
        grid=(T // tm, n_tiles + 1),
        in_specs=[pl.BlockSpec((tm, D), lambda i, j: (i, 0)),
                  pl.BlockSpec((te, D), lambda i, j: (jnp.minimum(j, n_tiles - 1), 0)),
                  pl.BlockSpec((te, D), lambda i, j: (jnp.maximum(j - 1, 0), 0)),
                  pl.BlockSpec((tm // tg, nr, tg, PEER_NKEYS), lambda i, j: (i, jnp.maximum(j - 1, 0), 0, 0)),
                  pl.BlockSpec((1, D), lambda i, j: (0, 0)),
                  pl.BlockSpec((1, D), lambda i, j: (0, 0))],
        out_specs=pl.BlockSpec((tm, D), lambda i, j: (i, 0)),
        out_shape=jax.ShapeDtypeStruct((T, D), F32),
        scratch_shapes=[pltpu.VMEM((tm, D), BF16), pltpu.VMEM((2, tm, te), F32), pltpu.VMEM((tm, te), BF16),
                        pltpu.VMEM((tm, D), F32)],
        compiler_params=_params("parallel", "arbitrary"),
        name="peer_ffn",
    )(x, u, v, gate_map, ln_g.reshape(1, D), ln_b.reshape(1, D))


def _rope_tables(S):
    inv = 1.0 / (ROPE_THETA ** (jnp.arange(0, HEAD_DIM, 2, dtype=F32) / HEAD_DIM))
    ang = jnp.arange(S, dtype=F32)[:, None] * inv[None, :]
    cos = jnp.cos(ang)
    sin = jnp.sin(ang)
    return jnp.concatenate([cos, cos], -1), jnp.concatenate([-sin, sin], -1)


def _retention_tables():
    H, C = RET_HEADS, RET_CHUNK
    log_g = jnp.log(1.0 - 2.0 ** (-5.0 - jnp.arange(H, dtype=F32)))
    i = jnp.arange(C, dtype=F32)
    diff = i[:, None] - i[None, :]
    causal = diff >= 0
    dmask = jnp.where(causal[None], jnp.exp(jnp.where(causal, diff, 0.0)[None] * log_g[:, None, None]), 0.0)
    xi = jnp.exp((i[None, :] + 1.0) * log_g[:, None])
    zeta = jnp.exp((C - 1.0 - i[None, :]) * log_g[:, None])
    cdec = jnp.exp(C * log_g)
    wide = (H, C, HEAD_DIM)
    return (dmask, jnp.broadcast_to(xi[:, :, None], wide), jnp.broadcast_to(zeta[:, :, None], wide),
            jnp.broadcast_to(cdec[:, None, None], (H, 1, HEAD_DIM)))


def _overlap_matrix(n_blk, S):
    n = np.arange(n_blk)[:, None] * CMP_STRIDE
    j = np.arange(LANES)[None, :] * SLC_BLOCK
    n_cmp = (S - CMP_BLOCK) // CMP_STRIDE + 1
    ov = (n < j + SLC_BLOCK) & (n + CMP_BLOCK > j) & (np.arange(n_blk)[:, None] < n_cmp) & (j < S)
    return jnp.asarray(ov, BF16)


def _expand_matrix(S):
    ex = (np.arange(S)[None, :] // SLC_BLOCK) == np.arange(LANES)[:, None]
    return jnp.asarray(ex, BF16)


def _pack_w_in(w_in, b_gate):
    D = w_in.shape[0]
    nsa_w = NSA_HEADS * HEAD_DIM
    kv_w = 6 * NSA_KV_HEADS * HEAD_DIM
    n_gate = 3 * NSA_HEADS
    per_group = n_gate // NSA_KV_HEADS
    gate_w = w_in[:, nsa_w + kv_w:nsa_w + kv_w + n_gate].reshape(D, NSA_KV_HEADS, per_group)
    gate_w = jnp.pad(gate_w, ((0, 0), (0, 0), (0, LANES - per_group))).reshape(D, NSA_KV_HEADS * LANES)
    pad_w = jnp.zeros((D, (CB_RQ - CB_PAD) * LANES), w_in.dtype)
    packed = jnp.concatenate([w_in[:, :nsa_w + kv_w], gate_w, pad_w, w_in[:, nsa_w + kv_w + n_gate:]], axis=1)
    bg = jnp.pad(b_gate.reshape(NSA_KV_HEADS, 1, per_group), ((0, 0), (0, 0), (0, LANES - per_group)))
    return packed.astype(BF16), bg


def _mixer(x2, B, S, w_in, b_gate, cmp_pos, cmp_w1, cmp_w2, gn_g, gn_b, w_out, ln_g, ln_b, alpha, tables):
    cos, sin, ret_tabs, expand = tables
    G = NSA_KV_HEADS
    w_packed, bg = _pack_w_in(w_in, b_gate)
    proj = matmul(x2, w_packed, tm=1024, tn=1024)
    n_blk = S // CMP_STRIDE
    kv = proj[:, CB_KC * LANES:CB_KS * LANES].reshape(B, n_blk, CMP_STRIDE, 2, G, HEAD_DIM)
    x16 = kv.transpose(3, 0, 4, 1, 2, 5).reshape(2, B * G, n_blk, CMP_STRIDE * HEAD_DIM)
    kv_cmp = nsa_compress(x16, cmp_pos.reshape(2, 1, CMP_BLOCK * HEAD_DIM), cmp_w1.astype(BF16),
                          cmp_w2.astype(BF16))
    o_cmp, sel = nsa_cmp_attention(proj, kv_cmp, _overlap_matrix(n_blk, S), B, S, tq=256)
    o_nsa = nsa_sel_win_attention(proj, bg, o_cmp, sel, cos, sin, expand, B, S, tq=128)
    o_ret = retention(proj, cos, sin, *ret_tabs, gn_g.reshape(1, -1), gn_b.reshape(1, -1), B, S)
    return matmul_residual_ln([o_nsa, o_ret], w_out.astype(BF16), x2, ln_g, ln_b, alpha, tm=512)


def _cross(x2, mem2, B, S, wq, wk, wv, wo, ln_g, ln_b, alpha):
    n_mem = mem2.shape[0] // B
    q = matmul(x2, wq.astype(BF16), tm=1024, tn=1024)
    kv = matmul(mem2, jnp.concatenate([wk, wv], axis=1).astype(BF16), tm=mem2.shape[0], tn=1024)
    o = cross_attention(q, kv, B, S, n_mem, tq=256)
    return matmul_residual_ln([o], wo.astype(BF16), x2, ln_g, ln_b, alpha, tm=512)


def _peer(x2, w_q, sub_keys, u_tab, v_tab, ln_g, ln_b, alpha):
    q = matmul(x2, w_q.astype(BF16), tm=1024, tn=1024)
    keys = sub_keys.reshape(PEER_HEADS * 2, PEER_NKEYS, PEER_HALF).astype(BF16)
    a, b, g = peer_topk(q, keys, tm=256)
    gate_map = peer_gate_map(a.T, b.T, g.T, tg=128)
    return peer_ffn(x2, u_tab.astype(BF16), v_tab.astype(BF16), gate_map, ln_g, ln_b, alpha, tm=512, nr=8)


def kernel(x, mem, w_in, b_gate, cmp_pos, cmp_w1, cmp_w2, ret_gn_g, ret_gn_b, w_mix_out, ln1_g, ln1_b,
           xa_wq, xa_wk, xa_wv, xa_wo, ln2_g, ln2_b, peer_wq, peer_sub_keys, peer_u, peer_v, ln3_g, ln3_b):
    B, S, D = x.shape
    depth = w_in.shape[0]
    alpha = (2 * depth) ** 0.25
    tables = (*_rope_tables(S), _retention_tables(), _expand_matrix(S))
    x2 = x.reshape(B * S, D)
    mem2 = mem.reshape(-1, D)
    for l in range(depth):
        x2 = _mixer(x2, B, S, w_in[l], b_gate[l], cmp_pos[l], cmp_w1[l], cmp_w2[l], ret_gn_g[l], ret_gn_b[l],
                    cast_layer_bf16(w_mix_out, l), ln1_g[l], ln1_b[l], alpha, tables)
        x2 = _cross(x2, mem2, B, S, cast_layer_bf16(xa_wq, l), xa_wk[l], xa_wv[l], cast_layer_bf16(xa_wo, l),
                    ln2_g[l], ln2_b[l], alpha)
        x2 = _peer(x2, cast_layer_bf16(peer_wq, l), peer_sub_keys[l], cast_layer_bf16(peer_u, l),
                   cast_layer_bf16(peer_v, l), ln3_g[l], ln3_b[l], alpha)
    return x2.reshape(B, S, D)
```

```python
import functools

import numpy as np
import jax
import jax.numpy as jnp
from jax import lax
from jax.experimental import pallas as pl
from jax.experimental.pallas import tpu as pltpu

F32 = jnp.float32
BF16 = jnp.bfloat16
I32 = jnp.int32

HEAD_DIM = 128
LANES = 128
NSA_HEADS = 8
NSA_KV_HEADS = 2
NSA_GROUP = NSA_HEADS // NSA_KV_HEADS
RET_HEADS = 8
CMP_BLOCK = 32
CMP_STRIDE = 16
SLC_BLOCK = 64
SLC_TOPK = 16
WINDOW = 512
RET_CHUNK = 128
XA_HEADS = 4
PEER_HEADS = 8
PEER_NKEYS = 128
PEER_HALF = 128
PEER_TOPK = 16
ROPE_THETA = 10000.0
LN_EPS = 1e-5
GN_EPS = 1e-5
NEG_INF = -1e30
FORCE_SCORE = 1e9
LOG2_E = 1.4426950408889634
VMEM_LIMIT = 56 * 1024 * 1024

CB_Q = 0
CB_KC, CB_VC, CB_KS, CB_VS, CB_KW, CB_VW = 8, 10, 12, 14, 16, 18
CB_GATE = 20
CB_PAD = 22
CB_RQ, CB_RK, CB_RV, CB_RG = 24, 32, 40, 48
P_PACKED = 56 * LANES


def _dot(a, b):
    return jnp.dot(a, b, preferred_element_type=F32)


def _dot_nt(a, b):
    return lax.dot_general(a, b, (((1,), (1,)), ((), ())), preferred_element_type=F32)


def _dot_tn(a, b):
    return lax.dot_general(a, b, (((0,), (0,)), ((), ())), preferred_element_type=F32)


def _gelu(x):
    return 0.5 * x * (1.0 + lax.erf(x * (2.0 ** -0.5)))


def _rope(x, cos, sin_signed):
    return x * cos + pltpu.roll(x, HEAD_DIM // 2, axis=1) * sin_signed


def _layer_norm(y, g, b):
    mu = jnp.mean(y, axis=-1, keepdims=True)
    d = y - mu
    var = jnp.mean(d * d, axis=-1, keepdims=True)
    return d * lax.rsqrt(var + LN_EPS) * g + b


def _softmax_pv(s, v):
    m = jnp.max(s, axis=-1, keepdims=True)
    e = jnp.exp(s - m)
    return _dot(e.astype(BF16), v) / jnp.sum(e, axis=-1, keepdims=True)


def _scaled_softmax_pv(s, scale, v):
    m = jnp.max(s, axis=-1, keepdims=True)
    e = jnp.exp2((s - m) * (scale * LOG2_E))
    return _dot(e.astype(BF16), v) / jnp.sum(e, axis=-1, keepdims=True)


def _params(*sem):
    return pltpu.CompilerParams(dimension_semantics=sem, vmem_limit_bytes=VMEM_LIMIT)


CAST_ROWS = 1024


def _cast_kernel(x_ref, o_ref):
    o_ref[...] = x_ref[...].astype(o_ref.dtype)


def cast_layer_bf16(stacked, l):
    _, R, C = stacked.shape
    tr = min(CAST_ROWS, R)
    return pl.pallas_call(
        _cast_kernel,
        grid=(R // tr,),
        in_specs=[pl.BlockSpec((None, tr, C), lambda i: (l, i, 0))],
        out_specs=pl.BlockSpec((tr, C), lambda i: (i, 0)),
        out_shape=jax.ShapeDtypeStruct((R, C), BF16),
        compiler_params=_params("parallel"),
        name="cast_bf16",
    )(stacked)


def _mm_kernel(a_ref, w_ref, o_ref, a_bf):
    @pl.when(pl.program_id(1) == 0)
    def _():
        a_bf[...] = a_ref[...].astype(BF16)

    o_ref[...] = _dot(a_bf[...], w_ref[...]).astype(o_ref.dtype)


def matmul(a, w, tm, tn):
    M, K = a.shape
    N = w.shape[1]
    assert M % tm == 0 and N % tn == 0
    return pl.pallas_call(
        _mm_kernel,
        grid=(M // tm, N // tn),
        in_specs=[pl.BlockSpec((tm, K), lambda i, j: (i, 0)),
                  pl.BlockSpec((K, tn), lambda i, j: (0, j))],
        out_specs=pl.BlockSpec((tm, tn), lambda i, j: (i, j)),
        out_shape=jax.ShapeDtypeStruct((M, N), F32),
        scratch_shapes=[pltpu.VMEM((tm, K), BF16)],
        compiler_params=_params("parallel", "arbitrary"),
        name="matmul",
    )(a, w)


def _mm_ln_kernel(n_in, alpha, *refs):
    a_refs = refs[:n_in]
    w_ref, x_ref, g_ref, b_ref, o_ref = refs[n_in:]
    acc = None
    off = 0
    for a_ref in a_refs:
        k = a_ref.shape[1]
        part = _dot(a_ref[...].astype(BF16), w_ref[off:off + k, :])
        acc = part if acc is None else acc + part
        off += k
    o_ref[...] = _layer_norm(alpha * x_ref[...] + acc, g_ref[...], b_ref[...])


def matmul_residual_ln(a_list, w, x, g, b, alpha, tm):
    M, D = x.shape
    n_in = len(a_list)
    in_specs = [pl.BlockSpec((tm, a.shape[1]), lambda i: (i, 0)) for a in a_list]
    in_specs += [pl.BlockSpec(w.shape, lambda i: (0, 0)),
                 pl.BlockSpec((tm, D), lambda i: (i, 0)),
                 pl.BlockSpec((1, D), lambda i: (0, 0)),
                 pl.BlockSpec((1, D), lambda i: (0, 0))]
    return pl.pallas_call(
        functools.partial(_mm_ln_kernel, n_in, alpha),
        grid=(M // tm,),
        in_specs=in_specs,
        out_specs=pl.BlockSpec((tm, D), lambda i: (i, 0)),
        out_shape=jax.ShapeDtypeStruct((M, D), F32),
        compiler_params=_params("parallel"),
        name="matmul_residual_ln",
    )(*a_list, w, x, g.reshape(1, D), b.reshape(1, D))


def _compress_kernel(x_ref, pos_ref, w1_ref, w2_ref, o_ref):
    x = x_ref[...]
    half = x.shape[1]
    pos = pos_ref[...]
    lo = (x + pos[:, :half]).astype(BF16)
    hi = (x + pos[:, half:]).astype(BF16)
    p_lo = _dot(lo, w1_ref[:half, :])
    p_hi = _dot(hi, w1_ref[half:, :])
    h = p_lo + pltpu.roll(p_hi, x.shape[0] - 1, axis=0)
    o_ref[...] = _dot(_gelu(h).astype(BF16), w2_ref[...])


def nsa_compress(x16, pos, w1, w2):
    _, BG, n_blk, wd = x16.shape
    return pl.pallas_call(
        _compress_kernel,
        grid=(2, BG),
        in_specs=[pl.BlockSpec((None, None, n_blk, wd), lambda i, j: (i, j, 0, 0)),
                  pl.BlockSpec((None, 1, 2 * wd), lambda i, j: (i, 0, 0)),
                  pl.BlockSpec((None, 2 * wd, HEAD_DIM), lambda i, j: (i, 0, 0)),
                  pl.BlockSpec((None, HEAD_DIM, HEAD_DIM), lambda i, j: (i, 0, 0))],
        out_specs=pl.BlockSpec((None, None, n_blk, HEAD_DIM), lambda i, j: (i, j, 0, 0)),
        out_shape=jax.ShapeDtypeStruct((2, BG, n_blk, HEAD_DIM), F32),
        compiler_params=_params("parallel", "parallel"),
        name="nsa_compress",
    )(x16, pos, w1, w2)


def _cmp_attn_kernel(tq, q_ref, kc_ref, vc_ref, ov_ref, o_ref, sel_ref):
    qi = pl.program_id(2)
    scale = HEAD_DIM ** -0.5
    t = qi * tq + lax.broadcasted_iota(I32, (tq, LANES), 0)
    lane = lax.broadcasted_iota(I32, (tq, LANES), 1)
    vis = (lane * CMP_STRIDE + (CMP_BLOCK - 1)) <= t
    kc = kc_ref[...].astype(BF16)
    vc = vc_ref[...].astype(BF16)
    psum = jnp.zeros((tq, LANES), F32)
    for r in range(NSA_GROUP):
        q = q_ref[:, r * HEAD_DIM:(r + 1) * HEAD_DIM].astype(BF16)
        s = _dot_nt(q, kc) * scale
        s = jnp.where(vis, s, NEG_INF)
        m = jnp.max(s, axis=-1, keepdims=True)
        e = jnp.where(vis, jnp.exp(s - m), 0.0)
        l = jnp.sum(e, axis=-1, keepdims=True)
        p = e / jnp.where(l > 0.0, l, 1.0)
        o_ref[:, r * HEAD_DIM:(r + 1) * HEAD_DIM] = _dot(p.astype(BF16), vc)
        psum = psum + p
    p_hi = psum.astype(BF16)
    p_lo = (psum - p_hi.astype(F32)).astype(BF16)
    n_slc = sel_ref.shape[0]
    imp_t = (_dot_nt(ov_ref[...], p_hi) + _dot_nt(ov_ref[...], p_lo))[:n_slc, :]
    blk = lax.broadcasted_iota(I32, (n_slc, tq), 0)
    t_col = qi * tq + lax.broadcasted_iota(I32, (n_slc, tq), 1)
    cur = t_col // SLC_BLOCK
    forced = (blk == 0) | (blk == cur) | (blk == cur - 1)
    score = jnp.where(forced, FORCE_SCORE, imp_t)
    score = jnp.where(blk * SLC_BLOCK <= t_col, score, -1.0)
    cnt = jnp.zeros((n_slc, tq), I32)
    for i in range(n_slc):
        ci = score[i:i + 1, :]
        before = (ci > score) | ((ci == score) & (blk > i))
        cnt = cnt + before.astype(I32)
    sel_ref[...] = (cnt < min(SLC_TOPK, n_slc)).astype(F32)


def nsa_cmp_attention(proj, kv_cmp, overlap, B, S, tq):
    M = proj.shape[0]
    nq = S // tq
    G = NSA_KV_HEADS
    gw = NSA_GROUP * HEAD_DIM
    n_blk = kv_cmp.shape[2]
    return pl.pallas_call(
        functools.partial(_cmp_attn_kernel, tq),
        grid=(B, G, nq),
        in_specs=[pl.BlockSpec((tq, gw), lambda b, g, i: (b * nq + i, g)),
                  pl.BlockSpec((None, None, n_blk, HEAD_DIM), lambda b, g, i: (0, b * G + g, 0, 0)),
                  pl.BlockSpec((None, None, n_blk, HEAD_DIM), lambda b, g, i: (1, b * G + g, 0, 0)),
                  pl.BlockSpec(overlap.shape, lambda b, g, i: (0, 0))],
        out_specs=[pl.BlockSpec((tq, gw), lambda b, g, i: (b * nq + i, g)),
                   pl.BlockSpec((None, S // SLC_BLOCK, tq), lambda b, g, i: (b * G + g, 0, i))],
        out_shape=[jax.ShapeDtypeStruct((M, NSA_HEADS * HEAD_DIM), F32),
                   jax.ShapeDtypeStruct((B * G, S // SLC_BLOCK, S), F32)],
        compiler_params=_params("parallel", "parallel", "parallel"),
        name="nsa_cmp_attention",
    )(proj, kv_cmp, kv_cmp, overlap)


SEL_PREFIX_CLASSES = 8


def _sel_win_kernel(tq, q_ref, gl_ref, bg_ref, oc_ref, sel_ref, ks_ref, vs_ref, kw_ref, vw_ref,
                    cos_ref, sin_ref, ex_ref, o_ref, ks_r, kw_r, vs_b, vw_b):
    qi = pl.program_id(2)
    nq = pl.num_programs(2)
    S = ks_ref.shape[0]
    scale = HEAD_DIM ** -0.5
    span = WINDOW + tq
    rows = NSA_GROUP * tq

    @pl.when(qi == 0)
    def _():
        cos = cos_ref[...]
        sin = sin_ref[...]
        ks_r[...] = _rope(ks_ref[...], cos, sin).astype(BF16)
        kw_r[...] = _rope(kw_ref[...], cos, sin).astype(BF16)
        vs_b[...] = vs_ref[...].astype(BF16)
        vw_b[...] = vw_ref[...].astype(BF16)

    q0 = pl.multiple_of(qi * tq, tq)
    cos_q = cos_ref[pl.ds(q0, tq), :]
    sin_q = sin_ref[pl.ds(q0, tq), :]
    q = jnp.concatenate(
        [_rope(q_ref[:, r * HEAD_DIM:(r + 1) * HEAD_DIM], cos_q, sin_q).astype(BF16) for r in range(NSA_GROUP)],
        axis=0)
    sel = jnp.concatenate([sel_ref[...].astype(BF16)] * NSA_GROUP, axis=1)
    gate = jax.nn.sigmoid(gl_ref[...] + bg_ref[...])

    w0 = pl.multiple_of(jnp.maximum(q0 - WINDOW, 0), tq)
    t_w = q0 + (lax.broadcasted_iota(I32, (rows, span), 0) & (tq - 1))
    k_w = w0 + lax.broadcasted_iota(I32, (rows, span), 1)
    mask_w = (k_w <= t_w) & (t_w - k_w < WINDOW)
    s = jnp.where(mask_w, _dot_nt(q, kw_r[pl.ds(w0, span), :]), NEG_INF)
    o_w = _scaled_softmax_pv(s, scale, vw_b[pl.ds(w0, span), :])

    def selected_branch(n_keys):
        sel_keys = _dot_tn(sel, ex_ref[:sel.shape[0], :n_keys]) > 0.5
        t_s = q0 + (lax.broadcasted_iota(I32, (rows, n_keys), 0) & (tq - 1))
        k_s = lax.broadcasted_iota(I32, (rows, n_keys), 1)
        s = jnp.where(sel_keys & (k_s <= t_s), _dot_nt(q, ks_r[:n_keys, :]), NEG_INF)
        o_s = _scaled_softmax_pv(s, scale, vs_b[:n_keys, :])
        for r in range(NSA_GROUP):
            cols = slice(r * HEAD_DIM, (r + 1) * HEAD_DIM)
            part = slice(r * tq, (r + 1) * tq)
            o_ref[:, cols] = (gate[:, 3 * r:3 * r + 1] * oc_ref[:, cols]
                              + gate[:, 3 * r + 1:3 * r + 2] * o_s[part]
                              + gate[:, 3 * r + 2:3 * r + 3] * o_w[part])

    per_class = nq // SEL_PREFIX_CLASSES
    for c in range(SEL_PREFIX_CLASSES):
        @pl.when(qi // per_class == c)
        def _(c=c):
            selected_branch((c + 1) * per_class * tq)


def nsa_sel_win_attention(proj, b_gate2, o_cmp, sel, cos, sin, expand, B, S, tq):
    M = proj.shape[0]
    nq = S // tq
    G = NSA_KV_HEADS
    gw = NSA_GROUP * HEAD_DIM

    def kv_spec(cb):
        return pl.BlockSpec((S, HEAD_DIM), lambda b, g, i: (b, cb + g))

    return pl.pallas_call(
        functools.partial(_sel_win_kernel, tq),
        grid=(B, G, nq),
        in_specs=[pl.BlockSpec((tq, gw), lambda b, g, i: (b * nq + i, g)),
                  pl.BlockSpec((tq, LANES), lambda b, g, i: (b * nq + i, CB_GATE + g)),
                  pl.BlockSpec((None, 1, LANES), lambda b, g, i: (g, 0, 0)),
                  pl.BlockSpec((tq, gw), lambda b, g, i: (b * nq + i, g)),
                  pl.BlockSpec((None, S // SLC_BLOCK, tq), lambda b, g, i: (b * G + g, 0, i)),
                  kv_spec(CB_KS), kv_spec(CB_VS), kv_spec(CB_KW), kv_spec(CB_VW),
                  pl.BlockSpec((S, HEAD_DIM), lambda b, g, i: (0, 0)),
                  pl.BlockSpec((S, HEAD_DIM), lambda b, g, i: (0, 0)),
                  pl.BlockSpec(expand.shape, lambda b, g, i: (0, 0))],
        out_specs=pl.BlockSpec((tq, gw), lambda b, g, i: (b * nq + i, g)),
        out_shape=jax.ShapeDtypeStruct((M, NSA_HEADS * HEAD_DIM), F32),
        scratch_shapes=[pltpu.VMEM((S, HEAD_DIM), BF16)] * 4,
        compiler_params=_params("parallel", "parallel", "arbitrary"),
        name="nsa_sel_win_attention",
    )(proj, proj, b_gate2, o_cmp, sel, proj, proj, proj, proj, cos, sin, expand)


def _retention_kernel(q_ref, k_ref, v_ref, gate_ref, cos_ref, sin_ref, dm_ref, xi_ref, zeta_ref,
                      cdec_ref, gng_ref, gnb_ref, o_ref, state):
    @pl.when(pl.program_id(1) == 0)
    def _():
        state[...] = jnp.zeros_like(state)

    cos = cos_ref[...]
    sin = sin_ref[...]
    for h in range(RET_HEADS):
        cols = slice(h * HEAD_DIM, (h + 1) * HEAD_DIM)
        q = _rope(q_ref[:, cols], cos, sin)
        k = _rope(k_ref[:, cols], cos, sin) * (HEAD_DIM ** -0.5)
        qb = q.astype(BF16)
        vb = v_ref[:, cols].astype(BF16)
        inner = _dot_nt(qb, k.astype(BF16)) * dm_ref[h]
        r_old = state[h]
        o = _dot(inner.astype(BF16), vb) + _dot(qb, r_old.astype(BF16)) * xi_ref[h]
        state[h] = r_old * cdec_ref[h] + _dot_tn((k * zeta_ref[h]).astype(BF16), vb)
        mu = jnp.mean(o, axis=-1, keepdims=True)
        d = o - mu
        var = jnp.mean(d * d, axis=-1, keepdims=True)
        y = d * lax.rsqrt(var + GN_EPS) * gng_ref[:, cols] + gnb_ref[:, cols]
        gate = gate_ref[:, cols]
        o_ref[:, cols] = gate * jax.nn.sigmoid(gate) * y


def retention(proj, cos, sin, dmask, xi, zeta, cdec, gn_g, gn_b, B, S):
    M = proj.shape[0]
    C = RET_CHUNK
    nc = S // C
    H = RET_HEADS
    W = H * HEAD_DIM

    def col_spec(cb):
        return pl.BlockSpec((C, W), lambda b, c: (b * nc + c, cb // H))

    def table_spec(rows):
        return pl.BlockSpec((H, rows, HEAD_DIM), lambda b, c: (0, 0, 0))

    return pl.pallas_call(
        _retention_kernel,
        grid=(B, nc),
        in_specs=[col_spec(CB_RQ), col_spec(CB_RK), col_spec(CB_RV), col_spec(CB_RG),
                  pl.BlockSpec((C, HEAD_DIM), lambda b, c: (c, 0)),
                  pl.BlockSpec((C, HEAD_DIM), lambda b, c: (c, 0)),
                  table_spec(C), table_spec(C), table_spec(C), table_spec(1),
                  pl.BlockSpec((1, W), lambda b, c: (0, 0)),
                  pl.BlockSpec((1, W), lambda b, c: (0, 0))],
        out_specs=pl.BlockSpec((C, W), lambda b, c: (b * nc + c, 0)),
        out_shape=jax.ShapeDtypeStruct((M, W), F32),
        scratch_shapes=[pltpu.VMEM((H, HEAD_DIM, HEAD_DIM), F32)],
        compiler_params=_params("parallel", "arbitrary"),
        name="retention",
    )(proj, proj, proj, proj, cos, sin, dmask, xi, zeta, cdec, gn_g, gn_b)


def _xattn_kernel(q_ref, k_ref, v_ref, o_ref):
    hd = q_ref.shape[1] // XA_HEADS
    scale = hd ** -0.5
    for h in range(XA_HEADS):
        cols = slice(h * hd, (h + 1) * hd)
        s = _dot_nt(q_ref[:, cols].astype(BF16), k_ref[:, cols].astype(BF16)) * scale
        o_ref[:, cols] = _softmax_pv(s, v_ref[:, cols].astype(BF16))


def cross_attention(q, kv, B, S, n_mem, tq):
    M, D = q.shape
    nq = S // tq
    return pl.pallas_call(
        _xattn_kernel,
        grid=(B, nq),
        in_specs=[pl.BlockSpec((tq, D), lambda b, i: (b * nq + i, 0)),
                  pl.BlockSpec((n_mem, D), lambda b, i: (b, 0)),
                  pl.BlockSpec((n_mem, D), lambda b, i: (b, 1))],
        out_specs=pl.BlockSpec((tq, D), lambda b, i: (b * nq + i, 0)),
        out_shape=jax.ShapeDtypeStruct((M, D), F32),
        compiler_params=_params("parallel", "parallel"),
        name="cross_attention",
    )(q, kv, kv)


def _top_k_rows(x, k, code):
    big = jnp.iinfo(jnp.int32).max
    vals, codes = [], []
    cur = x
    for _ in range(k):
        m = jnp.max(cur, axis=0, keepdims=True)
        c = jnp.min(jnp.where(cur == m, code, big), axis=0, keepdims=True)
        vals.append(m)
        codes.append(c)
        cur = jnp.where(code == c, -jnp.inf, cur)
    return jnp.concatenate(vals, axis=0), jnp.concatenate(codes, axis=0)


def _take_rows(table, idx, n):
    out = jnp.zeros(idx.shape, table.dtype)
    for p in range(n):
        out = jnp.where(idx == p, table[p:p + 1, :], out)
    return out


def _pair_candidates(s1, s2):
    K = PEER_TOPK
    tm = s1.shape[1]
    half = K // 2
    p_full = lax.broadcasted_iota(I32, (K, tm), 0)
    p_half = lax.broadcasted_iota(I32, (half, tm), 0)
    vals = [s1 + s2[0:1, :]]
    codes = [p_full * K]
    for q in range(1, half):
        n_valid = K // (q + 1)
        vals.append(jnp.where(p_half < n_valid, s1[:half, :] + s2[q:q + 1, :], -jnp.inf))
        codes.append(p_half * K + q)
    vals.append(s1[0:1, :] + s2[half:, :])
    codes.append(p_half + half)
    return jnp.concatenate(vals, axis=0), jnp.concatenate(codes, axis=0)


def _peer_topk_kernel(q_ref, keys_ref, a_ref, b_ref, g_ref):
    K = PEER_TOPK
    tm = q_ref.shape[0]
    key_row = lax.broadcasted_iota(I32, (PEER_NKEYS, tm), 0)
    for h in range(PEER_HEADS):
        halves = []
        for p in range(2):
            c0 = (2 * h + p) * PEER_HALF
            qhp = q_ref[:, c0:c0 + PEER_HALF].astype(BF16)
            s = _dot_nt(keys_ref[2 * h + p], qhp)
            halves.append(_top_k_rows(s, K, key_row))
        (s1, i1), (s2, i2) = halves
        cand, code = _pair_candidates(s1, s2)
        top, pos = _top_k_rows(cand, K, code)
        a_ref[h * K:(h + 1) * K, :] = _take_rows(i1, pos // K, K)
        b_ref[h * K:(h + 1) * K, :] = _take_rows(i2, pos % K, K)
        e = jnp.exp(top - jnp.max(top, axis=0, keepdims=True))
        g_ref[h * K:(h + 1) * K, :] = e / jnp.sum(e, axis=0, keepdims=True)


def peer_topk(q, keys, tm):
    T, D = q.shape
    HK = PEER_HEADS * PEER_TOPK
    out = jax.ShapeDtypeStruct((HK, T), I32)
    spec = pl.BlockSpec((HK, tm), lambda i: (0, i))
    return pl.pallas_call(
        _peer_topk_kernel,
        grid=(T // tm,),
        in_specs=[pl.BlockSpec((tm, D), lambda i: (i, 0)),
                  pl.BlockSpec(keys.shape, lambda i: (0, 0, 0))],
        out_specs=[spec, spec, spec],
        out_shape=[out, out, jax.ShapeDtypeStruct((HK, T), F32)],
        compiler_params=_params("parallel"),
        name="peer_topk",
    )(q, keys)


GATE_MAP_UNROLL = 8
GATE_MAP_TRIPS_UNROLLED = 4


def _peer_gate_map_kernel(tg, a_ref, b_ref, g_ref, o_ref):
    row = lax.broadcasted_iota(I32, (PEER_NKEYS, LANES), 0)

    def body(i, carry):
        t0 = pl.multiple_of(i * GATE_MAP_UNROLL, GATE_MAP_UNROLL)
        a8 = a_ref[pl.ds(t0, GATE_MAP_UNROLL), :]
        b8 = b_ref[pl.ds(t0, GATE_MAP_UNROLL), :]
        g8 = g_ref[pl.ds(t0, GATE_MAP_UNROLL), :]
        g8_hi = g8.astype(BF16).astype(F32)
        g8_lo = g8 - g8_hi
        maps = []
        for u in range(GATE_MAP_UNROLL):
            onehot_a = jnp.where(a8[u:u + 1, :] == row, 1.0, 0.0).astype(BF16)
            hit_b = b8[u:u + 1, :] == row
            b_hi = jnp.where(hit_b, g8_hi[u:u + 1, :], 0.0).astype(BF16)
            b_lo = jnp.where(hit_b, g8_lo[u:u + 1, :], 0.0).astype(BF16)
            maps.append(_dot_nt(jnp.concatenate([onehot_a, onehot_a], axis=1),
                                jnp.concatenate([b_hi, b_lo], axis=1)))
        o_ref[:, pl.ds(t0, GATE_MAP_UNROLL), :] = jnp.transpose(jnp.stack(maps, axis=0), (1, 0, 2))
        return carry

    lax.fori_loop(0, tg // GATE_MAP_UNROLL, body, 0, unroll=GATE_MAP_TRIPS_UNROLLED)


def peer_gate_map(a, b, g, tg):
    T, HK = a.shape
    spec = pl.BlockSpec((tg, HK), lambda i: (i, 0))
    return pl.pallas_call(
        functools.partial(_peer_gate_map_kernel, tg),
        grid=(T // tg,),
        in_specs=[spec, spec, spec],
        out_specs=pl.BlockSpec((None, PEER_NKEYS, tg, PEER_NKEYS), lambda i: (i, 0, 0, 0)),
        out_shape=jax.ShapeDtypeStruct((T // tg, PEER_NKEYS, tg, PEER_NKEYS), F32),
        compiler_params=_params("parallel"),
        name="peer_gate_map",
    )(a, b, g)


def _peer_ffn_kernel(alpha, x_ref, u_ref, v_ref, gm_ref, lg_ref, lb_ref, o_ref, x_bf, h_scr, w_bf, acc):
    j = pl.program_id(1)
    n_tiles = pl.num_programs(1) - 1
    n_sub, nr, tg, _ = gm_ref.shape
    slot = j % 2

    def project(dst):
        h_scr[dst] = _dot_nt(x_bf[...], u_ref[...])

    def gate_and_accumulate(src):
        for r in range(nr):
            cols = slice(r * PEER_NKEYS, (r + 1) * PEER_NKEYS)
            for s in range(n_sub):
                rows = slice(s * tg, (s + 1) * tg)
                w_bf[rows, cols] = (gm_ref[s, r] * _gelu(h_scr[src, rows, cols])).astype(BF16)
        acc[...] += _dot(w_bf[...], v_ref[...])

    @pl.when(j == 0)
    def _():
        x_bf[...] = x_ref[...].astype(BF16)
        acc[...] = jnp.zeros_like(acc)
        project(0)

    @pl.when((j > 0) & (j < n_tiles))
    def _():
        project(slot)
        gate_and_accumulate(1 - slot)

    @pl.when(j == n_tiles)
    def _():
        gate_and_accumulate(1 - slot)
        o_ref[...] = _layer_norm(alpha * x_ref[...] + acc[...], lg_ref[...], lb_ref[...])


def peer_ffn(x, u, v, gate_map, ln_g, ln_b, alpha, tm, nr):
    T, D = x.shape
    NE = u.shape[0]
    te = nr * PEER_NKEYS
    tg = gate_map.shape[2]
    n_tiles = NE // te
    return pl.pallas_call(
        functools.partial(_peer_ffn_kernel, alpha),
        grid=(T // tm, n_tiles + 1),
        in_specs=[pl.BlockSpec((tm, D), lambda i, j: (i, 0)),
                  pl.BlockSpec((te, D), lambda i, j: (jnp.minimum(j, n_tiles - 1), 0)),
                  pl.BlockSpec((te, D), lambda i, j: (jnp.maximum(j - 1, 0), 0)),
                  pl.BlockSpec((tm // tg, nr, tg, PEER_NKEYS), lambda i, j: (i, jnp.maximum(j - 1, 0), 0, 0)),
                  pl.BlockSpec((1, D), lambda i, j: (0, 0)),
                  pl.BlockSpec((1, D), lambda i, j: (0, 0))],
        out_specs=pl.BlockSpec((tm, D), lambda i, j: (i, 0)),
        out_shape=jax.ShapeDtypeStruct((T, D), F32),
        scratch_shapes=[pltpu.VMEM((tm, D), BF16), pltpu.VMEM((2, tm, te), F32), pltpu.VMEM((tm, te), BF16),
                        pltpu.VMEM((tm, D), F32)],
        compiler_params=_params("parallel", "arbitrary"),
        name="peer_ffn",
    )(x, u, v, gate_map, ln_g.reshape(1, D), ln_b.reshape(1, D))


def _rope_tables(S):
    inv = 1.0 / (ROPE_THETA ** (jnp.arange(0, HEAD_DIM, 2, dtype=F32) / HEAD_DIM))
    ang = jnp.arange(S, dtype=F32)[:, None] * inv[None, :]
    cos = jnp.cos(ang)
    sin = jnp.sin(ang)
    return jnp.concatenate([cos, cos], -1), jnp.concatenate([-sin, sin], -1)


def _retention_tables():
    H, C = RET_HEADS, RET_CHUNK
    log_g = jnp.log(1.0 - 2.0 ** (-5.0 - jnp.arange(H, dtype=F32)))
    i = jnp.arange(C, dtype=F32)
    diff = i[:, None] - i[None, :]
    causal = diff >= 0
    dmask = jnp.where(causal[None], jnp.exp(jnp.where(causal, diff, 0.0)[None] * log_g[:, None, None]), 0.0)
    xi = jnp.exp((i[None, :] + 1.0) * log_g[:, None])
    zeta = jnp.exp((C - 1.0 - i[None, :]) * log_g[:, None])
    cdec = jnp.exp(C * log_g)
    wide = (H, C, HEAD_DIM)
    return (dmask, jnp.broadcast_to(xi[:, :, None], wide), jnp.broadcast_to(zeta[:, :, None], wide),
            jnp.broadcast_to(cdec[:, None, None], (H, 1, HEAD_DIM)))


def _overlap_matrix(n_blk, S):
    n = np.arange(n_blk)[:, None] * CMP_STRIDE
    j = np.arange(LANES)[None, :] * SLC_BLOCK
    n_cmp = (S - CMP_BLOCK) // CMP_STRIDE + 1
    ov = (n < j + SLC_BLOCK) & (n + CMP_BLOCK > j) & (np.arange(n_blk)[:, None] < n_cmp) & (j < S)
    return jnp.asarray(ov, BF16)


def _expand_matrix(S):
    ex = (np.arange(S)[None, :] // SLC_BLOCK) == np.arange(LANES)[:, None]
    return jnp.asarray(ex, BF16)


def _pack_w_in(w_in, b_gate):
    D = w_in.shape[0]
    nsa_w = NSA_HEADS * HEAD_DIM
    kv_w = 6 * NSA_KV_HEADS * HEAD_DIM
    n_gate = 3 * NSA_HEADS
    per_group = n_gate // NSA_KV_HEADS
    gate_w = w_in[:, nsa_w + kv_w:nsa_w + kv_w + n_gate].reshape(D, NSA_KV_HEADS, per_group)
    gate_w = jnp.pad(gate_w, ((0, 0), (0, 0), (0, LANES - per_group))).reshape(D, NSA_KV_HEADS * LANES)
    pad_w = jnp.zeros((D, (CB_RQ - CB_PAD) * LANES), w_in.dtype)
    packed = jnp.concatenate([w_in[:, :nsa_w + kv_w], gate_w, pad_w, w_in[:, nsa_w + kv_w + n_gate:]], axis=1)
    bg = jnp.pad(b_gate.reshape(NSA_KV_HEADS, 1, per_group), ((0, 0), (0, 0), (0, LANES - per_group)))
    return packed.astype(BF16), bg


def _mixer(x2, B, S, w_in, b_gate, cmp_pos, cmp_w1, cmp_w2, gn_g, gn_b, w_out, ln_g, ln_b, alpha, tables):
    cos, sin, ret_tabs, expand = tables
    G = NSA_KV_HEADS
    w_packed, bg = _pack_w_in(w_in, b_gate)
    proj = matmul(x2, w_packed, tm=1024, tn=1024)
    n_blk = S // CMP_STRIDE
    kv = proj[:, CB_KC * LANES:CB_KS * LANES].reshape(B, n_blk, CMP_STRIDE, 2, G, HEAD_DIM)
    x16 = kv.transpose(3, 0, 4, 1, 2, 5).reshape(2, B * G, n_blk, CMP_STRIDE * HEAD_DIM)
    kv_cmp = nsa_compress(x16, cmp_pos.reshape(2, 1, CMP_BLOCK * HEAD_DIM), cmp_w1.astype(BF16),
                          cmp_w2.astype(BF16))
    o_cmp, sel = nsa_cmp_attention(proj, kv_cmp, _overlap_matrix(n_blk, S).T, B, S, tq=256)
    o_nsa = nsa_sel_win_attention(proj, bg, o_cmp, sel, cos, sin, expand, B, S, tq=128)
    o_ret = retention(proj, cos, sin, *ret_tabs, gn_g.reshape(1, -1), gn_b.reshape(1, -1), B, S)
    return matmul_residual_ln([o_nsa, o_ret], w_out.astype(BF16), x2, ln_g, ln_b, alpha, tm=512)


def _cross(x2, mem2, B, S, wq, wk, wv, wo, ln_g, ln_b, alpha):
    n_mem = mem2.shape[0] // B
    q = matmul(x2, wq.astype(BF16), tm=1024, tn=1024)
    kv = matmul(mem2, jnp.concatenate([wk, wv], axis=1).astype(BF16), tm=mem2.shape[0], tn=1024)
    o = cross_attention(q, kv, B, S, n_mem, tq=256)
    return matmul_residual_ln([o], wo.astype(BF16), x2, ln_g, ln_b, alpha, tm=512)


def _peer(x2, w_q, sub_keys, u_tab, v_tab, ln_g, ln_b, alpha):
    q = matmul(x2, w_q.astype(BF16), tm=1024, tn=1024)
    keys = sub_keys.reshape(PEER_HEADS * 2, PEER_NKEYS, PEER_HALF).astype(BF16)
    a, b, g = peer_topk(q, keys, tm=256)
    gate_map = peer_gate_map(a.T, b.T, g.T, tg=128)
    return peer_ffn(x2, u_tab.astype(BF16), v_tab.astype(BF16), gate_map, ln_g, ln_b, alpha, tm=512, nr=8)


def kernel(x, mem, w_in, b_gate, cmp_pos, cmp_w1, cmp_w2, ret_gn_g, ret_gn_b, w_mix_out, ln1_g, ln1_b,
           xa_wq, xa_wk, xa_wv, xa_wo, ln2_g, ln2_b, peer_wq, peer_sub_keys, peer_u, peer_v, ln3_g, ln3_b):
    B, S, D = x.shape
    depth = w_in.shape[0]
    alpha = (2 * depth) ** 0.25
    tables = (*_rope_tables(S), _retention_tables(), _expand_matrix(S))
    x2 = x.reshape(B * S, D)
    mem2 = mem.reshape(-1, D)
    for l in range(depth):
        x2 = _mixer(x2, B, S, w_in[l], b_gate[l], cmp_pos[l], cmp_w1[l], cmp_w2[l], ret_gn_g[l], ret_gn_b[l],
                    cast_layer_bf16(w_mix_out, l), ln1_g[l], ln1_b[l], alpha, tables)
        x2 = _cross(x2, mem2, B, S, cast_layer_bf16(xa_wq, l), xa_wk[l], xa_wv[l], cast_layer_bf16(xa_wo, l),
                    ln2_g[l], ln2_b[l], alpha)
        x2 = _peer(x2, cast_layer_bf16(peer_wq, l), peer_sub_keys[l], cast_layer_bf16(peer_u, l),
                   cast_layer_bf16(peer_v, l), ln3_g[l], ln3_b[l], alpha)
    return x2.reshape(B, S, D)
```
